```python
import math
import jax, jax.numpy as jnp
from jax import lax
import numpy as np

D_MODEL = 1024
BATCH = 4
SEQ = 8192
DEPTH = 1

N_META = 16
BLOCK = 128
PAD_FRONT = BLOCK - N_META
NORM_EPS = 1e-6
NEG = -1e30

SSD_HEADDIM = 64
SSD_HEADS = 16
SSD_INNER = SSD_HEADS * SSD_HEADDIM
SSD_GROUPS = 2
SSD_STATE = 128
SSD_CONV = 4
SSD_CONV_DIM = SSD_INNER + 2 * SSD_GROUPS * SSD_STATE

ATT_HEADS = 8
ATT_HEAD_DIM = 64
ATT_QK = ATT_HEADS * 2 * ATT_HEAD_DIM
ATT_V = ATT_HEADS * 2 * ATT_HEAD_DIM

D_FF = ((8 * D_MODEL // 3 + 255) // 256) * 256

SPLIT_SIZES = [SSD_INNER,
               SSD_CONV_DIM,
               SSD_HEADS,
               ATT_QK, ATT_QK, ATT_V,
               2 * D_MODEL]
IN_COLS = sum(SPLIT_SIZES)
SPLIT_POINTS = list(np.cumsum(SPLIT_SIZES)[:-1])

kernel_name = "hybrid_ssd_diffattn_gated_block"


def rmsnorm(x, g):
    xf = x.astype(jnp.float32)
    y = xf * lax.rsqrt(jnp.mean(xf * xf, axis=-1, keepdims=True) + NORM_EPS)
    return (y * g.astype(jnp.float32)).astype(x.dtype)


def causal_depthwise_conv(u, w, b):
    c = u.shape[-1]
    kern = jnp.transpose(w)[:, None, :].astype(u.dtype)
    y = lax.conv_general_dilated(u, kern, window_strides=(1,), padding=[(SSD_CONV - 1, 0)],
                                 dimension_numbers=('NWC', 'WIO', 'NWC'), feature_group_count=c)
    return y + b.astype(u.dtype)


def ssd_chunked(xs, bmat, cmat, dt_raw, dt_bias, a_log, d_skip, valid):
    bsz, L = xs.shape[0], xs.shape[1]
    nc = L // BLOCK
    hpg = SSD_HEADS // SSD_GROUPS
    dt = jax.nn.softplus(dt_raw.astype(jnp.float32) + dt_bias.astype(jnp.float32))
    dt = jnp.where(valid[None, :, None], dt, 0.0)
    a = -jnp.exp(a_log.astype(jnp.float32))
    da = (dt * a).reshape(bsz, nc, BLOCK, SSD_GROUPS, hpg)
    dt = dt.reshape(bsz, nc, BLOCK, SSD_GROUPS, hpg)
    x = xs.astype(jnp.float32).reshape(bsz, nc, BLOCK, SSD_GROUPS, hpg, SSD_HEADDIM)
    xdt = x * dt[..., None]
    bc = bmat.astype(jnp.float32).reshape(bsz, nc, BLOCK, SSD_GROUPS, SSD_STATE)
    cc = cmat.astype(jnp.float32).reshape(bsz, nc, BLOCK, SSD_GROUPS, SSD_STATE)
    acs = jnp.cumsum(da, axis=2)

    tri = jnp.arange(BLOCK)[:, None] >= jnp.arange(BLOCK)[None, :]
    seg = acs[:, :, :, None] - acs[:, :, None, :]
    decay = jnp.exp(jnp.where(tri[:, :, None, None], seg, -jnp.inf))
    cb = jnp.einsum('bclgn,bcsgn->bclsg', cc, bc)
    y_diag = jnp.einsum('bclsg,bclsge,bcsgep->bclgep', cb, decay, xdt)

    decay_states = jnp.exp(acs[:, :, -1:] - acs)
    states = jnp.einsum('bclgn,bclge,bclgep->bcgepn', bc, decay_states, xdt)
    chunk_decay = jnp.exp(acs[:, :, -1])

    def step(h, inp):
        s_c, d_c = inp
        return h * d_c[..., None, None] + s_c, h

    h0 = jnp.zeros((bsz, SSD_GROUPS, hpg, SSD_HEADDIM, SSD_STATE), jnp.float32)
    _, prev = lax.scan(step, h0, (jnp.moveaxis(states, 1, 0), jnp.moveaxis(chunk_decay, 1, 0)))
    prev = jnp.moveaxis(prev, 0, 1)
    y_off = jnp.einsum('bclgn,bcgepn,bclge->bclgep', cc, prev, jnp.exp(acs))

    y = y_diag + y_off + x * d_skip.astype(jnp.float32).reshape(SSD_GROUPS, hpg)[:, :, None]
    return y.reshape(bsz, L, SSD_INNER)


def diff_attention(q, k, v, lam, lam_init, subln_g):
    bsz, L = q.shape[0], q.shape[1]
    nb = L // BLOCK
    slopes = 2.0 ** (-8.0 * (jnp.arange(ATT_HEADS, dtype=jnp.float32) + 1.0) / ATT_HEADS)
    scale = ATT_HEAD_DIM ** -0.5
    kpos = jnp.arange(L)
    kvalid = kpos >= PAD_FRONT
    vf = v.astype(jnp.float32)
    qb = jnp.moveaxis(q.reshape(bsz, nb, BLOCK, ATT_HEADS, 2, ATT_HEAD_DIM), 1, 0)

    def block(args):
        qblk, i = args
        qpos = i * BLOCK + jnp.arange(BLOCK)
        s = jnp.einsum('bqhcd,bkhcd->bhcqk', qblk, k,
                       preferred_element_type=jnp.float32) * scale
        dist = (qpos[:, None] - kpos[None, :]).astype(jnp.float32)
        allowed = (kpos[None, :] <= qpos[:, None]) & kvalid[None, :]
        s = s - slopes[None, :, None, None, None] * dist
        s = jnp.where(allowed, s, NEG)
        p = jax.nn.softmax(s, axis=-1)
        a = p[:, :, 0] - lam * p[:, :, 1]
        return jnp.einsum('bhqk,bkhe->bqhe', a, vf)

    out = lax.map(block, (qb, jnp.arange(nb)))
    out = jnp.moveaxis(out, 0, 1).reshape(bsz, L, ATT_HEADS, 2 * ATT_HEAD_DIM)
    out = rmsnorm(out, subln_g) * (1.0 - lam_init)
    return out.reshape(bsz, L, ATT_V)


def setup_inputs(seed: int = 0) -> dict:
    key = jax.random.key(seed)
    ks = jax.random.split(key, 24)
    f32 = jnp.float32
    nrm = lambda k, shp, s: jax.random.normal(k, shp, f32) * s
    dt0 = jnp.exp(jax.random.uniform(ks[9], (DEPTH, SSD_HEADS), f32)
                  * (math.log(0.1) - math.log(0.001)) + math.log(0.001))
    return {
        "x": nrm(ks[0], (BATCH, SEQ, D_MODEL), 1.0),
        "meta_tokens": nrm(ks[1], (N_META, D_MODEL), 1.0),
        "norm_mix_g": 1.0 + nrm(ks[2], (DEPTH, D_MODEL), 0.02),
        "w_in": nrm(ks[3], (DEPTH, D_MODEL, IN_COLS), D_MODEL ** -0.5),
        "gate_bias": nrm(ks[4], (DEPTH, 2 * D_MODEL), 0.01),
        "conv_w": nrm(ks[5], (DEPTH, SSD_CONV_DIM, SSD_CONV), SSD_CONV ** -0.5),
        "conv_b": nrm(ks[6], (DEPTH, SSD_CONV_DIM), 0.01),
        "dt_bias": dt0 + jnp.log(-jnp.expm1(-dt0)),
        "a_log": jnp.log(jax.random.uniform(ks[7], (DEPTH, SSD_HEADS), f32, 1.0, 16.0)),
        "d_skip": 1.0 + nrm(ks[8], (DEPTH, SSD_HEADS), 0.1),
        "ssd_norm_g": 1.0 + nrm(ks[10], (DEPTH, SSD_INNER), 0.02),
        "lambda_q1": nrm(ks[11], (DEPTH, ATT_HEAD_DIM), 0.1),
        "lambda_k1": nrm(ks[12], (DEPTH, ATT_HEAD_DIM), 0.1),
        "lambda_q2": nrm(ks[13], (DEPTH, ATT_HEAD_DIM), 0.1),
        "lambda_k2": nrm(ks[14], (DEPTH, ATT_HEAD_DIM), 0.1),
        "subln_g": 1.0 + nrm(ks[15], (DEPTH, 2 * ATT_HEAD_DIM), 0.02),
        "w_ssd_branch": nrm(ks[16], (DEPTH, SSD_INNER, D_MODEL), SSD_INNER ** -0.5),
        "w_attn_branch": nrm(ks[17], (DEPTH, ATT_V, D_MODEL), ATT_V ** -0.5),
        "w_out": nrm(ks[18], (DEPTH, D_MODEL, D_MODEL), D_MODEL ** -0.5),
        "norm_ffn_g": 1.0 + nrm(ks[19], (DEPTH, D_MODEL), 0.02),
        "w_gate_ffn": nrm(ks[20], (DEPTH, D_MODEL, D_FF), D_MODEL ** -0.5),
        "w_up_ffn": nrm(ks[21], (DEPTH, D_MODEL, D_FF), D_MODEL ** -0.5),
        "w_down_ffn": nrm(ks[22], (DEPTH, D_FF, D_MODEL), D_FF ** -0.5),
        "norm_final_g": 1.0 + nrm(ks[23], (D_MODEL,), 0.02),
    }


def reference(x, meta_tokens, norm_mix_g, w_in, gate_bias, conv_w, conv_b, dt_bias, a_log,
              d_skip, ssd_norm_g, lambda_q1, lambda_k1, lambda_q2, lambda_k2, subln_g,
              w_ssd_branch, w_attn_branch, w_out, norm_ffn_g, w_gate_ffn, w_up_ffn,
              w_down_ffn, norm_final_g):
    bsz = x.shape[0]
    dt_ = x.dtype
    meta = jnp.broadcast_to(meta_tokens.astype(dt_)[None], (bsz, N_META, D_MODEL))
    h = jnp.concatenate([jnp.zeros((bsz, PAD_FRONT, D_MODEL), dt_), meta, x], axis=1)
    L = h.shape[1]
    valid = jnp.arange(L) >= PAD_FRONT
    vmask = valid.astype(dt_)[None, :, None]

    for l in range(DEPTH):
        u = rmsnorm(h, norm_mix_g[l]) * vmask
        proj = u @ w_in[l]
        z, xbc, dt_raw, q, k, v, gates = jnp.split(proj, SPLIT_POINTS, axis=-1)

        xbc = jax.nn.silu(causal_depthwise_conv(xbc, conv_w[l], conv_b[l]))
        xs, bm, cm = jnp.split(xbc, [SSD_INNER, SSD_INNER + SSD_GROUPS * SSD_STATE], axis=-1)
        bm = bm.reshape(bsz, L, SSD_GROUPS, SSD_STATE)
        cm = cm.reshape(bsz, L, SSD_GROUPS, SSD_STATE)
        y_ssd = ssd_chunked(xs, bm, cm, dt_raw, dt_bias[l], a_log[l], d_skip[l], valid)
        y_ssd = rmsnorm(y_ssd * jax.nn.silu(z.astype(jnp.float32)), ssd_norm_g[l]).astype(dt_)

        lam_init = 0.8 - 0.6 * math.exp(-0.3 * l)
        lam = (jnp.exp(jnp.sum(lambda_q1[l].astype(jnp.float32) * lambda_k1[l].astype(jnp.float32)))
               - jnp.exp(jnp.sum(lambda_q2[l].astype(jnp.float32) * lambda_k2[l].astype(jnp.float32)))
               + lam_init)
        q = q.reshape(bsz, L, ATT_HEADS, 2, ATT_HEAD_DIM)
        k = k.reshape(bsz, L, ATT_HEADS, 2, ATT_HEAD_DIM)
        v = v.reshape(bsz, L, ATT_HEADS, 2 * ATT_HEAD_DIM)
        y_att = diff_attention(q, k, v, lam, lam_init, subln_g[l]).astype(dt_)

        g_ssd, g_att = jnp.split(jax.nn.sigmoid(gates + gate_bias[l]), 2, axis=-1)
        merged = g_ssd * (y_ssd @ w_ssd_branch[l]) + g_att * (y_att @ w_attn_branch[l])
        h = h + merged @ w_out[l]

        u2 = rmsnorm(h, norm_ffn_g[l])
        h = h + (jax.nn.silu(u2 @ w_gate_ffn[l]) * (u2 @ w_up_ffn[l])) @ w_down_ffn[l]

    out = rmsnorm(h, norm_final_g)
    return out[:, PAD_FRONT + N_META:]
```

```python
import functools
import math

import jax
import jax.numpy as jnp
from jax import lax
from jax.experimental import pallas as pl
from jax.experimental.pallas import tpu as pltpu

F32 = jnp.float32
BF16 = jnp.bfloat16

N_META = 16
BLOCK = 128
PAD_FRONT = BLOCK - N_META
NORM_EPS = 1e-6
NEG = -1e30

SSD_HEADDIM = 64
SSD_HEADS = 16
SSD_INNER = SSD_HEADS * SSD_HEADDIM
SSD_GROUPS = 2
SSD_STATE = 128
SSD_CONV = 4
SSD_BC = SSD_GROUPS * SSD_STATE
SSD_CONV_DIM = SSD_INNER + 2 * SSD_BC
HEADS_PER_GROUP = SSD_HEADS // SSD_GROUPS
GROUP_INNER = HEADS_PER_GROUP * SSD_HEADDIM

ATT_HEADS = 8
ATT_HEAD_DIM = 64
ATT_QK = ATT_HEADS * 2 * ATT_HEAD_DIM
ATT_V = ATT_HEADS * 2 * ATT_HEAD_DIM
ATT_HEAD_COLS = 2 * ATT_HEAD_DIM

LOG2E = math.log2(math.e)
Q_SCALE = ATT_HEAD_DIM ** -0.5 * LOG2E

LANES = 128
VMEM_LIMIT_BYTES = 56 * 1024 * 1024
INPROJ_ROWS = 512
ATTN_Q_ROWS = 512
ATTN_K_ROWS = 512
FFN_ROWS = 256


def _resident(shape):
    zeros = (0,) * len(shape)
    return pl.BlockSpec(shape, lambda *_: zeros, pipeline_mode=pl.Buffered(1))


def _sigmoid(x):
    return 1.0 / (1.0 + jnp.exp(-x))


def _silu(x):
    return x * _sigmoid(x)


def _dot(a, b):
    return jnp.dot(a, b, preferred_element_type=F32)


def _dot_nt(a, b):
    return lax.dot_general(a, b, (((1,), (1,)), ((), ())), preferred_element_type=F32)


def _dot_tn(a, b):
    return lax.dot_general(a, b, (((0,), (0,)), ((), ())), preferred_element_type=F32)


def _inproj_kernel(x_ref, g_ref, wq_ref, wk_ref, wg_ref, wxbct_ref, wzt_ref, wvt_ref, wdtt_ref,
                   q_ref, k_ref, gates_ref, xbct_ref, zt_ref, vt_ref, dtt_ref, *, masked_rows):
    x = x_ref[0]
    ms = jnp.mean(x * x, axis=-1, keepdims=True)
    u = x * lax.rsqrt(ms + NORM_EPS) * g_ref[...]
    if masked_rows:
        row = lax.broadcasted_iota(jnp.int32, u.shape, 0)
        u = jnp.where(row >= masked_rows, u, 0.0)
    ub = u.astype(BF16)
    q_ref[0] = (_dot(ub, wq_ref[...]) * Q_SCALE).astype(q_ref.dtype)
    k_ref[0] = _dot(ub, wk_ref[...]).astype(k_ref.dtype)
    gates_ref[0] = _dot(ub, wg_ref[...]).astype(gates_ref.dtype)
    xbct_ref[0] = _dot_nt(wxbct_ref[...], ub)
    zt_ref[0] = _dot_nt(wzt_ref[...], ub).astype(zt_ref.dtype)
    vt_ref[0] = _dot_nt(wvt_ref[...], ub).astype(vt_ref.dtype)
    dtt_ref[0] = _dot_nt(wdtt_ref[...], ub)


def _in_projection(h, g, w, *, rows, masked_rows):
    nb, seq, d = h.shape
    grid = (nb, seq // rows)
    row_major = lambda cols, dt: (jax.ShapeDtypeStruct((nb, seq, cols), dt),
                                  pl.BlockSpec((1, rows, cols), lambda b, i: (b, i, 0)))
    chan_major = lambda cols, dt: (jax.ShapeDtypeStruct((nb, cols, seq), dt),
                                   pl.BlockSpec((1, cols, rows), lambda b, i: (b, 0, i)))
    outs = [row_major(ATT_QK, BF16), row_major(ATT_QK, BF16), row_major(2 * d, BF16),
            chan_major(SSD_CONV_DIM, F32), chan_major(SSD_INNER, BF16), chan_major(ATT_V, BF16),
            chan_major(SSD_HEADS, F32)]
    weights = [w["q"], w["k"], w["gates"], w["xbc_t"], w["z_t"], w["v_t"], w["dt_t"]]
    return pl.pallas_call(
        functools.partial(_inproj_kernel, masked_rows=masked_rows),
        grid=grid,
        in_specs=[pl.BlockSpec((1, rows, d), lambda b, i: (b, i, 0)), _resident(g.shape)]
                 + [_resident(wi.shape) for wi in weights],
        out_specs=[o[1] for o in outs],
        out_shape=[o[0] for o in outs],
        compiler_params=pltpu.CompilerParams(
            dimension_semantics=("parallel", "parallel"), vmem_limit_bytes=VMEM_LIMIT_BYTES),
        name="in_projection",
    )(h, g, *weights)


def _expand_heads(a):
    n = a.shape[1]
    return jnp.concatenate(
        [jnp.broadcast_to(a[h:h + 1, :], (SSD_HEADDIM, n)) for h in range(SSD_HEADS)], axis=0)


def _ssd_chunk(x_raw, x_prev, dt_raw, state, p, *, masked_steps, want_y):
    conv_w, conv_b, dt_bias, a_log, d_skip = p
    lane = lax.broadcasted_iota(jnp.int32, x_raw.shape, 1)
    acc = conv_b + conv_w[SSD_CONV - 1] * x_raw
    for j in range(1, SSD_CONV):
        joined = jnp.where(lane >= BLOCK - j, x_prev, x_raw)
        acc = acc + conv_w[SSD_CONV - 1 - j] * pltpu.roll(joined, j, 1)
    xbc = _silu(acc)
    xs = xs_f32 = xbc[:SSD_INNER]
    b_all = xbc[SSD_INNER:SSD_INNER + SSD_BC].astype(BF16)
    c_all = xbc[SSD_INNER + SSD_BC:].astype(BF16)

    v = dt_raw + dt_bias
    dt = jnp.maximum(v, 0.0) + jnp.log1p(jnp.exp(-jnp.abs(v)))
    if masked_steps:
        step = lax.broadcasted_iota(jnp.int32, dt.shape, 1)
        dt = jnp.where(step >= masked_steps, dt, 0.0)
    da = dt * (-jnp.exp(a_log))
    r = lax.broadcasted_iota(jnp.int32, (BLOCK, BLOCK), 0)
    c = lax.broadcasted_iota(jnp.int32, (BLOCK, BLOCK), 1)
    upper = r <= c
    acs = jnp.dot(da, upper.astype(F32), preferred_element_type=F32,
                  precision=lax.Precision.HIGHEST)
    last = acs[:, BLOCK - 1:BLOCK]
    to_end = jnp.exp(last - acs) * dt
    chunk_decay = jnp.broadcast_to(jnp.exp(last), acs.shape)

    xdt_end = (xs * _expand_heads(to_end)).astype(BF16)
    state_decay = _expand_heads(chunk_decay)
    new_state = []
    y_parts = []
    if want_y:
        acs_col = jnp.concatenate([acs, jnp.zeros((BLOCK - SSD_HEADS, BLOCK), F32)], axis=0).T
        grow = jnp.exp(acs)
    for g in range(SSD_GROUPS):
        bg = b_all[g * SSD_STATE:(g + 1) * SSD_STATE]
        cg = c_all[g * SSD_STATE:(g + 1) * SSD_STATE]
        lo, hi = g * GROUP_INNER, (g + 1) * GROUP_INNER
        st = state[lo:hi]
        new_state.append(st * state_decay[lo:hi] + _dot_nt(xdt_end[lo:hi], bg))
        if want_y:
            cb_t = _dot_tn(bg, cg)
            y_off = _dot(st.astype(BF16), cg)
            for e in range(HEADS_PER_GROUP):
                h = g * HEADS_PER_GROUP + e
                rows = slice(h * SSD_HEADDIM, (h + 1) * SSD_HEADDIM)
                seg = acs[h:h + 1, :] - acs_col[:, h:h + 1]
                decay = jnp.where(upper, jnp.exp(seg), 0.0)
                g_t = (cb_t * decay).astype(BF16)
                xdt = (xs[rows] * dt[h:h + 1, :]).astype(BF16)
                y_h = _dot(xdt, g_t)
                y_h = y_h + y_off[e * SSD_HEADDIM:(e + 1) * SSD_HEADDIM] * grow[h:h + 1, :]
                y_parts.append(y_h)
    state = jnp.concatenate(new_state, axis=0)
    if not want_y:
        return None, state
    y = jnp.concatenate(y_parts, axis=0) + xs_f32 * d_skip
    return y, state


def _ssd_kernel(xbct_ref, dtt_ref, zt_ref, xbct_meta_ref, dtt_meta_ref, convw_ref, convb_ref,
                dtb_ref, alog_ref, dskip_ref, ng_ref, y_ref, state_ref, prev_ref):
    c = pl.program_id(1)
    p = ([convw_ref[k] for k in range(SSD_CONV)], convb_ref[...], dtb_ref[...], alog_ref[...],
         dskip_ref[...])

    @pl.when(c == 0)
    def _():
        x_meta = xbct_meta_ref[0]
        _, st = _ssd_chunk(x_meta, jnp.zeros_like(x_meta), dtt_meta_ref[0],
                           jnp.zeros(state_ref.shape, F32), p,
                           masked_steps=PAD_FRONT, want_y=False)
        state_ref[...] = st
        prev_ref[...] = x_meta

    x_raw = xbct_ref[0]
    y, st = _ssd_chunk(x_raw, prev_ref[...], dtt_ref[0], state_ref[...], p,
                       masked_steps=0, want_y=True)
    state_ref[...] = st
    prev_ref[...] = x_raw
    y = y * _silu(zt_ref[0].astype(F32))
    ms = jnp.mean(y * y, axis=0, keepdims=True)
    y = y * lax.rsqrt(ms + NORM_EPS) * ng_ref[...]
    y_ref[0] = y.T.astype(y_ref.dtype)


def _ssd(xbct, dtt, zt, xbct_meta, dtt_meta, params):
    nb, _, seq = xbct.shape
    chunk = lambda rows: pl.BlockSpec((1, rows, BLOCK), lambda b, c: (b, 0, c))
    return pl.pallas_call(
        _ssd_kernel,
        grid=(nb, seq // BLOCK),
        in_specs=[chunk(SSD_CONV_DIM), chunk(SSD_HEADS), chunk(SSD_INNER),
                  _resident(xbct_meta.shape), _resident(dtt_meta.shape)]
                 + [_resident(a.shape) for a in params],
        out_specs=pl.BlockSpec((1, BLOCK, SSD_INNER), lambda b, c: (b, c, 0)),
        out_shape=jax.ShapeDtypeStruct((nb, seq, SSD_INNER), BF16),
        scratch_shapes=[pltpu.VMEM((SSD_INNER, SSD_STATE), F32),
                        pltpu.VMEM((SSD_CONV_DIM, BLOCK), F32)],
        compiler_params=pltpu.CompilerParams(
            dimension_semantics=("parallel", "arbitrary"), vmem_limit_bytes=VMEM_LIMIT_BYTES),
        name="ssd",
    )(xbct, dtt, zt, xbct_meta, dtt_meta, *params)


def _attn_kernel(q_ref, k_ref, vt_ref, k_meta_ref, vt_meta_ref, slope_ref, lam_ref, g_ref,
                 o_ref, m_ref, l_ref, acc_ref, *, lam_init):
    i = pl.program_id(2)
    tq, tk = ATTN_Q_ROWS, ATTN_K_ROWS
    qf = q_ref[0]
    lane = lax.broadcasted_iota(jnp.int32, qf.shape, 1)
    zero = jnp.zeros_like(qf)
    q_comp = (jnp.where(lane < ATT_HEAD_DIM, qf, zero), jnp.where(lane >= ATT_HEAD_DIM, qf, zero))
    slope = slope_ref[0][:, 0:1]
    key_row = lax.broadcasted_iota(jnp.int32, (tk, tq), 0)
    qry_col = lax.broadcasted_iota(jnp.int32, (tk, tq), 1)
    bias = key_row.astype(F32) * slope
    meta_row = lax.broadcasted_iota(jnp.int32, (BLOCK, tq), 0)
    meta_bias = meta_row.astype(F32) * slope

    m_ref[...] = jnp.full(m_ref.shape, NEG, F32)
    l_ref[...] = jnp.zeros(l_ref.shape, F32)
    acc_ref[...] = jnp.zeros(acc_ref.shape, F32)

    def tile(kt, vtt, key_start, tile_bias, mask):
        sigma = slope * lax.convert_element_type(key_start, F32)
        for comp in range(2):
            s = _dot_nt(kt, q_comp[comp]) + tile_bias
            if mask is not None:
                s = jnp.where(mask, s, NEG)
            m_old = m_ref[comp]
            m_new = jnp.maximum(m_old, jnp.max(s, axis=0, keepdims=True) + sigma)
            alpha = jnp.exp2(m_old - m_new)
            pr = jnp.exp2(s - (m_new - sigma))
            l_ref[comp] = alpha * l_ref[comp] + jnp.sum(pr, axis=0, keepdims=True)
            acc_ref[comp] = alpha * acc_ref[comp] + _dot(vtt, pr.astype(BF16))
            m_ref[comp] = m_new

    q_start = BLOCK + i * tq
    tile(k_meta_ref[0], vt_meta_ref[0], -q_start, meta_bias, meta_row >= PAD_FRONT)

    def body(j, carry):
        start = pl.multiple_of(j * tk, tk)
        tile(k_ref[0, pl.ds(start, tk), :], vt_ref[0, :, pl.ds(start, tk)], (j - i) * tk, bias,
             None)
        return carry

    lax.fori_loop(0, i, body, 0)
    start = pl.multiple_of(i * tk, tk)
    tile(k_ref[0, pl.ds(start, tk), :], vt_ref[0, :, pl.ds(start, tk)], 0, bias,
         key_row <= qry_col)

    lam = (jnp.exp(jnp.sum(lam_ref[0:1] * lam_ref[1:2], axis=-1, keepdims=True))
           - jnp.exp(jnp.sum(lam_ref[2:3] * lam_ref[3:4], axis=-1, keepdims=True)) + lam_init)
    o = acc_ref[0] * (1.0 / l_ref[0]) - lam * (acc_ref[1] * (1.0 / l_ref[1]))
    ms = jnp.mean(o * o, axis=0, keepdims=True)
    y = o * lax.rsqrt(ms + NORM_EPS) * g_ref[...] * (1.0 - lam_init)
    o_ref[0] = y.T.astype(o_ref.dtype)


def _attention(q, k, vt, k_meta, vt_meta, slopes, lam_vecs, subln_g, *, lam_init):
    nb, seq, _ = q.shape
    tq = ATTN_Q_ROWS
    hc = ATT_HEAD_COLS
    return pl.pallas_call(
        functools.partial(_attn_kernel, lam_init=lam_init),
        grid=(nb, ATT_HEADS, seq // tq),
        in_specs=[pl.BlockSpec((1, tq, hc), lambda b, h, i: (b, i, h)),
                  pl.BlockSpec((1, seq, hc), lambda b, h, i: (b, 0, h)),
                  pl.BlockSpec((1, hc, seq), lambda b, h, i: (b, h, 0)),
                  pl.BlockSpec((1, BLOCK, hc), lambda b, h, i: (0, 0, h)),
                  pl.BlockSpec((1, hc, BLOCK), lambda b, h, i: (0, h, 0)),
                  pl.BlockSpec((1, 1, LANES), lambda b, h, i: (h, 0, 0)),
                  _resident(lam_vecs.shape), _resident(subln_g.shape)],
        out_specs=pl.BlockSpec((1, tq, hc), lambda b, h, i: (b, i, h)),
        out_shape=jax.ShapeDtypeStruct((nb, seq, ATT_V), BF16),
        scratch_shapes=[pltpu.VMEM((2, 1, tq), F32), pltpu.VMEM((2, 1, tq), F32),
                        pltpu.VMEM((2, hc, tq), F32)],
        compiler_params=pltpu.CompilerParams(
            dimension_semantics=("parallel", "parallel", "arbitrary"),
            vmem_limit_bytes=VMEM_LIMIT_BYTES),
        name="diff_attention",
    )(q, k, vt, k_meta, vt_meta, slopes, lam_vecs, subln_g)


def _merge_ffn_kernel(x_ref, ys_ref, ya_ref, gates_ref, gbias_ref, ws_ref, wa_ref, wo_ref,
                      nffn_ref, wg_ref, wu_ref, wd_ref, nfin_ref, o_ref):
    d = x_ref.shape[-1]
    gates = _sigmoid(gates_ref[...].astype(F32) + gbias_ref[...])
    merged = (gates[:, :d] * _dot(ys_ref[...], ws_ref[...])
              + gates[:, d:] * _dot(ya_ref[...], wa_ref[...]))
    h = x_ref[...] + _dot(merged.astype(BF16), wo_ref[...])
    u = h * lax.rsqrt(jnp.mean(h * h, axis=-1, keepdims=True) + NORM_EPS) * nffn_ref[...]
    ub = u.astype(BF16)
    hidden = _silu(_dot(ub, wg_ref[...])) * _dot(ub, wu_ref[...])
    h = h + _dot(hidden.astype(BF16), wd_ref[...])
    o_ref[...] = h * lax.rsqrt(jnp.mean(h * h, axis=-1, keepdims=True) + NORM_EPS) * nfin_ref[...]


def _merge_ffn(x2, ys, ya, gates, consts):
    n, d = x2.shape
    rows = FFN_ROWS
    tile = lambda cols: pl.BlockSpec((rows, cols), lambda i: (i, 0))
    gbias, ws, wa, wo, nffn, wg, wu, wd, nfin = consts
    return pl.pallas_call(
        _merge_ffn_kernel,
        grid=(n // rows,),
        in_specs=[tile(d), tile(ys.shape[1]), tile(ya.shape[1]), tile(gates.shape[1])]
                 + [_resident(a.shape) for a in consts],
        out_specs=tile(d),
        out_shape=jax.ShapeDtypeStruct((n, d), F32),
        compiler_params=pltpu.CompilerParams(
            dimension_semantics=("parallel",), vmem_limit_bytes=VMEM_LIMIT_BYTES),
        name="merge_ffn",
    )(x2, ys, ya, gates, *consts)


def _lane_bcast(v, n=LANES):
    return jnp.broadcast_to(v.astype(F32)[:, None], (v.shape[0], n))


def _layer(x, meta_chunk, l, norm_mix_g, w_in, gate_bias, conv_w, conv_b, dt_bias, a_log, d_skip,
           ssd_norm_g, lam_vecs, subln_g, w_ssd_branch, w_attn_branch, w_out, norm_ffn_g,
           w_gate_ffn, w_up_ffn, w_down_ffn, norm_final_g):
    nb, seq, d = x.shape
    sizes = [SSD_INNER, SSD_CONV_DIM, SSD_HEADS, ATT_QK, ATT_QK, ATT_V, 2 * d]
    offs = [0]
    for s in sizes:
        offs.append(offs[-1] + s)
    col = lambda idx: w_in[:, offs[idx]:offs[idx + 1]].astype(BF16)
    w = {"z_t": col(0).T, "xbc_t": col(1).T, "dt_t": col(2).T, "q": col(3), "k": col(4),
         "v_t": col(5).T, "gates": col(6)}
    g_mix = norm_mix_g.astype(F32)[None, :]

    q, k, gates, xbct, zt, vt, dtt = _in_projection(
        x, g_mix, w, rows=INPROJ_ROWS, masked_rows=0)
    _, k_meta, _, xbct_meta, _, vt_meta, dtt_meta = _in_projection(
        meta_chunk, g_mix, w, rows=BLOCK, masked_rows=PAD_FRONT)

    ssd_params = [
        jnp.stack([_lane_bcast(conv_w[:, t]) for t in range(SSD_CONV)]),
        _lane_bcast(conv_b), _lane_bcast(dt_bias), _lane_bcast(a_log),
        _lane_bcast(jnp.repeat(d_skip, SSD_HEADDIM)), _lane_bcast(ssd_norm_g)]
    y_ssd = _ssd(xbct, dtt, zt, xbct_meta, dtt_meta, ssd_params)

    lam_init = 0.8 - 0.6 * math.exp(-0.3 * l)
    heads = jnp.arange(ATT_HEADS, dtype=F32)
    slopes = 2.0 ** (-8.0 * (heads + 1.0) / ATT_HEADS) * LOG2E
    slopes = jnp.broadcast_to(slopes[:, None, None], (ATT_HEADS, 1, LANES))
    y_att = _attention(q, k, vt, k_meta, vt_meta, slopes, lam_vecs.astype(F32),
                       _lane_bcast(subln_g, ATTN_Q_ROWS), lam_init=lam_init)

    consts = [gate_bias.astype(F32)[None, :], w_ssd_branch.astype(BF16),
              w_attn_branch.astype(BF16), w_out.astype(BF16), norm_ffn_g.astype(F32)[None, :],
              w_gate_ffn.astype(BF16), w_up_ffn.astype(BF16), w_down_ffn.astype(BF16),
              norm_final_g.astype(F32)[None, :]]
    out = _merge_ffn(x.reshape(nb * seq, d), y_ssd.reshape(nb * seq, SSD_INNER),
                     y_att.reshape(nb * seq, ATT_V), gates.reshape(nb * seq, 2 * d), consts)
    return out.reshape(nb, seq, d)


def kernel(x, meta_tokens, norm_mix_g, w_in, gate_bias, conv_w, conv_b, dt_bias, a_log, d_skip, ssd_norm_g, lambda_q1, lambda_k1, lambda_q2, lambda_k2, subln_g, w_ssd_branch, w_attn_branch, w_out, norm_ffn_g, w_gate_ffn, w_up_ffn, w_down_ffn, norm_final_g):
    depth = w_in.shape[0]
    assert depth == 1, "the fused final norm assumes a single layer"
    d = x.shape[-1]
    meta_chunk = jnp.concatenate(
        [jnp.zeros((PAD_FRONT, d), x.dtype), meta_tokens.astype(x.dtype)], axis=0)[None]
    l = 0
    lam_vecs = jnp.stack([lambda_q1[l], lambda_k1[l], lambda_q2[l], lambda_k2[l]])
    return _layer(x, meta_chunk, l, norm_mix_g[l], w_in[l], gate_bias[l], conv_w[l], conv_b[l],
                  dt_bias[l], a_log[l], d_skip[l], ssd_norm_g[l], lam_vecs, subln_g[l],
                  w_ssd_branch[l], w_attn_branch[l], w_out[l], norm_ffn_g[l], w_gate_ffn[l],
                  w_up_ffn[l], w_down_ffn[l], norm_final_g)
```

```python
import functools
import math

import jax
import jax.numpy as jnp
from jax import lax
from jax.experimental import pallas as pl
from jax.experimental.pallas import tpu as pltpu

F32 = jnp.float32
BF16 = jnp.bfloat16

N_META = 16
BLOCK = 128
PAD_FRONT = BLOCK - N_META
NORM_EPS = 1e-6
NEG = -1e30

SSD_HEADDIM = 64
SSD_HEADS = 16
SSD_INNER = SSD_HEADS * SSD_HEADDIM
SSD_GROUPS = 2
SSD_STATE = 128
SSD_CONV = 4
SSD_BC = SSD_GROUPS * SSD_STATE
SSD_CONV_DIM = SSD_INNER + 2 * SSD_BC
HEADS_PER_GROUP = SSD_HEADS // SSD_GROUPS
GROUP_INNER = HEADS_PER_GROUP * SSD_HEADDIM

ATT_HEADS = 8
ATT_HEAD_DIM = 64
ATT_QK = ATT_HEADS * 2 * ATT_HEAD_DIM
ATT_V = ATT_HEADS * 2 * ATT_HEAD_DIM
ATT_HEAD_COLS = 2 * ATT_HEAD_DIM

LOG2E = math.log2(math.e)
Q_SCALE = ATT_HEAD_DIM ** -0.5 * LOG2E

LANES = 128
VMEM_LIMIT_BYTES = 56 * 1024 * 1024
INPROJ_ROWS = 512
ATTN_Q_ROWS = 512
ATTN_K_ROWS = 512
POS_RADIX = 16
DENOM_ROWS = 16
FFN_ROWS = 256


def _resident(shape):
    zeros = (0,) * len(shape)
    return pl.BlockSpec(shape, lambda *_: zeros, pipeline_mode=pl.Buffered(1))


def _sigmoid(x):
    return 1.0 / (1.0 + jnp.exp(-x))


def _silu(x):
    return x * _sigmoid(x)


def _dot(a, b):
    return jnp.dot(a, b, preferred_element_type=F32)


def _dot_nt(a, b):
    return lax.dot_general(a, b, (((1,), (1,)), ((), ())), preferred_element_type=F32)


def _dot_tn(a, b):
    return lax.dot_general(a, b, (((0,), (0,)), ((), ())), preferred_element_type=F32)


def _inproj_kernel(x_ref, g_ref, wq_ref, wk_ref, wg_ref, wxbct_ref, wzt_ref, wvt_ref, wdtt_ref,
                   q_ref, k_ref, gates_ref, xbct_ref, zt_ref, vt_ref, dtt_ref, *, masked_rows):
    x = x_ref[0]
    ms = jnp.mean(x * x, axis=-1, keepdims=True)
    u = x * lax.rsqrt(ms + NORM_EPS) * g_ref[...]
    if masked_rows:
        row = lax.broadcasted_iota(jnp.int32, u.shape, 0)
        u = jnp.where(row >= masked_rows, u, 0.0)
    ub = u.astype(BF16)
    q_ref[0] = (_dot(ub, wq_ref[...]) * Q_SCALE).astype(q_ref.dtype)
    k_ref[0] = _dot(ub, wk_ref[...]).astype(k_ref.dtype)
    gates_ref[0] = _dot(ub, wg_ref[...]).astype(gates_ref.dtype)
    xbct_ref[0] = _dot_nt(wxbct_ref[...], ub)
    zt_ref[0] = _dot_nt(wzt_ref[...], ub).astype(zt_ref.dtype)
    vt_ref[0] = _dot_nt(wvt_ref[...], ub).astype(vt_ref.dtype)
    dtt_ref[0] = _dot_nt(wdtt_ref[...], ub)


def _in_projection(h, g, w, *, rows, masked_rows):
    nb, seq, d = h.shape
    grid = (nb, seq // rows)
    row_major = lambda cols, dt: (jax.ShapeDtypeStruct((nb, seq, cols), dt),
                                  pl.BlockSpec((1, rows, cols), lambda b, i: (b, i, 0)))
    chan_major = lambda cols, dt: (jax.ShapeDtypeStruct((nb, cols, seq), dt),
                                   pl.BlockSpec((1, cols, rows), lambda b, i: (b, 0, i)))
    outs = [row_major(ATT_QK, BF16), row_major(ATT_QK, BF16), row_major(2 * d, BF16),
            chan_major(SSD_CONV_DIM, F32), chan_major(SSD_INNER, BF16), chan_major(ATT_V, BF16),
            chan_major(SSD_HEADS, F32)]
    weights = [w["q"], w["k"], w["gates"], w["xbc_t"], w["z_t"], w["v_t"], w["dt_t"]]
    return pl.pallas_call(
        functools.partial(_inproj_kernel, masked_rows=masked_rows),
        grid=grid,
        in_specs=[pl.BlockSpec((1, rows, d), lambda b, i: (b, i, 0)), _resident(g.shape)]
                 + [_resident(wi.shape) for wi in weights],
        out_specs=[o[1] for o in outs],
        out_shape=[o[0] for o in outs],
        compiler_params=pltpu.CompilerParams(
            dimension_semantics=("parallel", "parallel"), vmem_limit_bytes=VMEM_LIMIT_BYTES),
        name="in_projection",
    )(h, g, *weights)


def _expand_heads(a):
    n = a.shape[1]
    return jnp.concatenate(
        [jnp.broadcast_to(a[h:h + 1, :], (SSD_HEADDIM, n)) for h in range(SSD_HEADS)], axis=0)


def _ssd_chunk(x_raw, x_prev, dt_raw, state, p, *, masked_steps, want_y):
    conv_w, conv_b, dt_bias, a_log, d_skip = p
    lane = lax.broadcasted_iota(jnp.int32, x_raw.shape, 1)
    acc = conv_b + conv_w[SSD_CONV - 1] * x_raw
    for j in range(1, SSD_CONV):
        joined = jnp.where(lane >= BLOCK - j, x_prev, x_raw)
        acc = acc + conv_w[SSD_CONV - 1 - j] * pltpu.roll(joined, j, 1)
    xbc = _silu(acc)
    xs = xs_f32 = xbc[:SSD_INNER]
    b_all = xbc[SSD_INNER:SSD_INNER + SSD_BC].astype(BF16)
    c_all = xbc[SSD_INNER + SSD_BC:].astype(BF16)

    v = dt_raw + dt_bias
    dt = jnp.maximum(v, 0.0) + jnp.log1p(jnp.exp(-jnp.abs(v)))
    if masked_steps:
        step = lax.broadcasted_iota(jnp.int32, dt.shape, 1)
        dt = jnp.where(step >= masked_steps, dt, 0.0)
    da = dt * (-jnp.exp(a_log))
    r = lax.broadcasted_iota(jnp.int32, (BLOCK, BLOCK), 0)
    c = lax.broadcasted_iota(jnp.int32, (BLOCK, BLOCK), 1)
    upper = r <= c
    acs = jnp.dot(da, upper.astype(F32), preferred_element_type=F32,
                  precision=lax.Precision.HIGHEST)
    last = acs[:, BLOCK - 1:BLOCK]
    to_end = jnp.exp(last - acs) * dt
    chunk_decay = jnp.broadcast_to(jnp.exp(last), acs.shape)

    xdt_end = (xs * _expand_heads(to_end)).astype(BF16)
    state_decay = _expand_heads(chunk_decay)
    new_state = []
    y_parts = []
    if want_y:
        acs_col = jnp.concatenate([acs, jnp.zeros((BLOCK - SSD_HEADS, BLOCK), F32)], axis=0).T
        grow = jnp.exp(acs)
    for g in range(SSD_GROUPS):
        bg = b_all[g * SSD_STATE:(g + 1) * SSD_STATE]
        cg = c_all[g * SSD_STATE:(g + 1) * SSD_STATE]
        lo, hi = g * GROUP_INNER, (g + 1) * GROUP_INNER
        st = state[lo:hi]
        new_state.append(st * state_decay[lo:hi] + _dot_nt(xdt_end[lo:hi], bg))
        if want_y:
            cb_t = _dot_tn(bg, cg)
            y_off = _dot(st.astype(BF16), cg)
            for e in range(HEADS_PER_GROUP):
                h = g * HEADS_PER_GROUP + e
                rows = slice(h * SSD_HEADDIM, (h + 1) * SSD_HEADDIM)
                seg = acs[h:h + 1, :] - acs_col[:, h:h + 1]
                decay = jnp.where(upper, jnp.exp(seg), 0.0)
                g_t = (cb_t * decay).astype(BF16)
                xdt = (xs[rows] * dt[h:h + 1, :]).astype(BF16)
                y_h = _dot(xdt, g_t)
                y_h = y_h + y_off[e * SSD_HEADDIM:(e + 1) * SSD_HEADDIM] * grow[h:h + 1, :]
                y_parts.append(y_h)
    state = jnp.concatenate(new_state, axis=0)
    if not want_y:
        return None, state
    y = jnp.concatenate(y_parts, axis=0) + xs_f32 * d_skip
    return y, state


def _ssd_kernel(xbct_ref, dtt_ref, zt_ref, xbct_meta_ref, dtt_meta_ref, convw_ref, convb_ref,
                dtb_ref, alog_ref, dskip_ref, ng_ref, y_ref, state_ref, prev_ref):
    c = pl.program_id(1)
    p = ([convw_ref[k] for k in range(SSD_CONV)], convb_ref[...], dtb_ref[...], alog_ref[...],
         dskip_ref[...])

    @pl.when(c == 0)
    def _():
        x_meta = xbct_meta_ref[0]
        _, st = _ssd_chunk(x_meta, jnp.zeros_like(x_meta), dtt_meta_ref[0],
                           jnp.zeros(state_ref.shape, F32), p,
                           masked_steps=PAD_FRONT, want_y=False)
        state_ref[...] = st
        prev_ref[...] = x_meta

    x_raw = xbct_ref[0]
    y, st = _ssd_chunk(x_raw, prev_ref[...], dtt_ref[0], state_ref[...], p,
                       masked_steps=0, want_y=True)
    state_ref[...] = st
    prev_ref[...] = x_raw
    y = y * _silu(zt_ref[0].astype(F32))
    ms = jnp.mean(y * y, axis=0, keepdims=True)
    y = y * lax.rsqrt(ms + NORM_EPS) * ng_ref[...]
    y_ref[0] = y.T.astype(y_ref.dtype)


def _ssd(xbct, dtt, zt, xbct_meta, dtt_meta, params):
    nb, _, seq = xbct.shape
    chunk = lambda rows: pl.BlockSpec((1, rows, BLOCK), lambda b, c: (b, 0, c))
    return pl.pallas_call(
        _ssd_kernel,
        grid=(nb, seq // BLOCK),
        in_specs=[chunk(SSD_CONV_DIM), chunk(SSD_HEADS), chunk(SSD_INNER),
                  _resident(xbct_meta.shape), _resident(dtt_meta.shape)]
                 + [_resident(a.shape) for a in params],
        out_specs=pl.BlockSpec((1, BLOCK, SSD_INNER), lambda b, c: (b, c, 0)),
        out_shape=jax.ShapeDtypeStruct((nb, seq, SSD_INNER), BF16),
        scratch_shapes=[pltpu.VMEM((SSD_INNER, SSD_STATE), F32),
                        pltpu.VMEM((SSD_CONV_DIM, BLOCK), F32)],
        compiler_params=pltpu.CompilerParams(
            dimension_semantics=("parallel", "arbitrary"), vmem_limit_bytes=VMEM_LIMIT_BYTES),
        name="ssd",
    )(xbct, dtt, zt, xbct_meta, dtt_meta, *params)


def _attn_kernel(q_ref, k_ref, vt_ref, k_meta_ref, vt_meta_ref, slope_ref, coef_ref, pos_ref,
                 lam_ref, g_ref, o_ref, m_ref, acc_ref, sa_ref, ca_ref, sb_ref, cb_ref, *,
                 lam_init):
    i = pl.program_id(2)
    tq, tk = ATTN_Q_ROWS, ATTN_K_ROWS
    qf = q_ref[0]
    lane = lax.broadcasted_iota(jnp.int32, qf.shape, 1)
    zero = jnp.zeros_like(qf)
    coef = jnp.broadcast_to(coef_ref[0], (tq, LANES)).astype(BF16)
    q_comp = tuple(jnp.concatenate([jnp.where(keep, qf, zero), coef], axis=1)
                   for keep in (lane < ATT_HEAD_DIM, lane >= ATT_HEAD_DIM))
    slope = slope_ref[0][:, 0:1]
    key_row = lax.broadcasted_iota(jnp.int32, (tk, tq), 0)
    qry_col = lax.broadcasted_iota(jnp.int32, (tk, tq), 1)
    meta_row = lax.broadcasted_iota(jnp.int32, (BLOCK, tq), 0)

    m_ref[...] = jnp.full(m_ref.shape, NEG, F32)
    acc_ref[...] = jnp.zeros(acc_ref.shape, F32)

    def scores(kt, mask):
        k_aug = jnp.concatenate([kt, pos_ref[0:kt.shape[0], :]], axis=1)
        out = []
        for comp in range(2):
            s = _dot_nt(k_aug, q_comp[comp])
            out.append(s if mask is None else jnp.where(mask, s, NEG))
        return out

    def fold(s_of, cmax_of, vtt, key_start):
        sigma = slope * lax.convert_element_type(key_start, F32)
        v_aug = jnp.concatenate([vtt, jnp.ones((DENOM_ROWS, vtt.shape[1]), BF16)], axis=0)
        for comp in range(2):
            m_old = m_ref[comp]
            m_new = jnp.maximum(m_old, cmax_of(comp) + sigma)
            alpha = jnp.exp2(m_old - m_new)
            pr = jnp.exp2(s_of(comp) - (m_new - sigma))
            acc_ref[comp] = alpha * acc_ref[comp] + _dot(v_aug, pr.astype(BF16))
            m_ref[comp] = m_new

    def key_tile(j):
        return k_ref[0, pl.ds(pl.multiple_of(j * tk, tk), tk), :]

    def value_tile(j):
        return vt_ref[0, :, pl.ds(pl.multiple_of(j * tk, tk), tk)]

    def stage(j, mask, stage_ref, cmax_ref):
        for comp, s in enumerate(scores(key_tile(j), mask)):
            stage_ref[comp] = s
            cmax_ref[comp] = jnp.max(s, axis=0, keepdims=True)

    def fold_staged(j, stage_ref, cmax_ref):
        fold(lambda comp: stage_ref[comp], lambda comp: cmax_ref[comp], value_tile(j),
             (j - i) * tk)

    slots = ((sa_ref, ca_ref), (sb_ref, cb_ref))
    stage(i, key_row <= qry_col, *slots[1])

    def run(start, n):
        prev = jnp.where(start == 0, i, start - 1)
        for r in range(n):
            stage(start + r, None, *slots[r % 2])
            fold_staged(prev, *slots[1 - r % 2])
            prev = start + r

    def quad(u, carry):
        run(4 * u, 4)
        return carry

    lax.fori_loop(0, i // 4, quad, 0)

    @pl.when((i & 2) != 0)
    def _():
        run((i // 4) * 4, 2)

    @pl.when((i & 1) != 0)
    def _():
        run(i - 1, 1)

    last = jnp.maximum(i - 1, 0)
    for parity in range(2):
        @pl.when((i & 1) == parity)
        def _():
            fold_staged(last, *slots[1 - parity])

    q_start = BLOCK + i * tq
    s_meta = scores(k_meta_ref[0], meta_row >= PAD_FRONT)
    fold(lambda comp: s_meta[comp], lambda comp: jnp.max(s_meta[comp], axis=0, keepdims=True),
         vt_meta_ref[0], -q_start)

    lam = (jnp.exp(jnp.sum(lam_ref[0:1] * lam_ref[1:2], axis=-1, keepdims=True))
           - jnp.exp(jnp.sum(lam_ref[2:3] * lam_ref[3:4], axis=-1, keepdims=True)) + lam_init)
    hc = ATT_HEAD_COLS
    out = [acc_ref[comp, 0:hc, :] * (1.0 / acc_ref[comp, hc:hc + 1, :]) for comp in range(2)]
    o = out[0] - lam * out[1]
    ms = jnp.mean(o * o, axis=0, keepdims=True)
    y = o * lax.rsqrt(ms + NORM_EPS) * g_ref[...] * (1.0 - lam_init)
    o_ref[0] = y.T.astype(o_ref.dtype)


def _alibi_operands():
    heads = jnp.arange(ATT_HEADS, dtype=F32)
    slope = 2.0 ** (-8.0 * (heads + 1.0) / ATT_HEADS) * LOG2E
    pieces, rest = [], slope
    for _ in range(3):
        piece = rest.astype(BF16).astype(F32)
        pieces.append(piece)
        rest = rest - piece
    cols = [POS_RADIX * p for p in pieces] + pieces
    coef = jnp.zeros((ATT_HEADS, 1, LANES), F32)
    coef = coef.at[:, 0, :len(cols)].set(jnp.stack(cols, axis=1))
    row = jnp.arange(ATTN_K_ROWS)
    pos = jnp.zeros((ATTN_K_ROWS, LANES), F32)
    pos = pos.at[:, 0:3].set((row // POS_RADIX).astype(F32)[:, None])
    pos = pos.at[:, 3:6].set((row % POS_RADIX).astype(F32)[:, None])
    slope = jnp.broadcast_to(slope[:, None, None], (ATT_HEADS, 1, LANES))
    return slope, coef, pos.astype(BF16)


def _attention(q, k, vt, k_meta, vt_meta, lam_vecs, subln_g, *, lam_init):
    nb, seq, _ = q.shape
    tq = ATTN_Q_ROWS
    hc = ATT_HEAD_COLS
    slopes, coef, pos = _alibi_operands()
    return pl.pallas_call(
        functools.partial(_attn_kernel, lam_init=lam_init),
        grid=(nb, ATT_HEADS, seq // tq),
        in_specs=[pl.BlockSpec((1, tq, hc), lambda b, h, i: (b, i, h)),
                  pl.BlockSpec((1, seq, hc), lambda b, h, i: (b, 0, h)),
                  pl.BlockSpec((1, hc, seq), lambda b, h, i: (b, h, 0)),
                  pl.BlockSpec((1, BLOCK, hc), lambda b, h, i: (0, 0, h)),
                  pl.BlockSpec((1, hc, BLOCK), lambda b, h, i: (0, h, 0)),
                  pl.BlockSpec((1, 1, LANES), lambda b, h, i: (h, 0, 0)),
                  pl.BlockSpec((1, 1, LANES), lambda b, h, i: (h, 0, 0)),
                  _resident(pos.shape), _resident(lam_vecs.shape), _resident(subln_g.shape)],
        out_specs=pl.BlockSpec((1, tq, hc), lambda b, h, i: (b, i, h)),
        out_shape=jax.ShapeDtypeStruct((nb, seq, ATT_V), BF16),
        scratch_shapes=[pltpu.VMEM((2, 1, tq), F32), pltpu.VMEM((2, hc + DENOM_ROWS, tq), F32)]
                       + [pltpu.VMEM((2, ATTN_K_ROWS, tq), F32), pltpu.VMEM((2, 1, tq), F32)] * 2,
        compiler_params=pltpu.CompilerParams(
            dimension_semantics=("parallel", "parallel", "arbitrary"),
            vmem_limit_bytes=VMEM_LIMIT_BYTES),
        name="diff_attention",
    )(q, k, vt, k_meta, vt_meta, slopes, coef, pos, lam_vecs, subln_g)


def _merge_ffn_kernel(x_ref, ys_ref, ya_ref, gates_ref, gbias_ref, ws_ref, wa_ref, wo_ref,
                      nffn_ref, wg_ref, wu_ref, wd_ref, nfin_ref, o_ref):
    d = x_ref.shape[-1]
    gates = _sigmoid(gates_ref[...].astype(F32) + gbias_ref[...])
    merged = (gates[:, :d] * _dot(ys_ref[...], ws_ref[...])
              + gates[:, d:] * _dot(ya_ref[...], wa_ref[...]))
    h = x_ref[...] + _dot(merged.astype(BF16), wo_ref[...])
    u = h * lax.rsqrt(jnp.mean(h * h, axis=-1, keepdims=True) + NORM_EPS) * nffn_ref[...]
    ub = u.astype(BF16)
    hidden = _silu(_dot(ub, wg_ref[...])) * _dot(ub, wu_ref[...])
    h = h + _dot(hidden.astype(BF16), wd_ref[...])
    o_ref[...] = h * lax.rsqrt(jnp.mean(h * h, axis=-1, keepdims=True) + NORM_EPS) * nfin_ref[...]


def _merge_ffn(x2, ys, ya, gates, consts):
    n, d = x2.shape
    rows = FFN_ROWS
    tile = lambda cols: pl.BlockSpec((rows, cols), lambda i: (i, 0))
    gbias, ws, wa, wo, nffn, wg, wu, wd, nfin = consts
    return pl.pallas_call(
        _merge_ffn_kernel,
        grid=(n // rows,),
        in_specs=[tile(d), tile(ys.shape[1]), tile(ya.shape[1]), tile(gates.shape[1])]
                 + [_resident(a.shape) for a in consts],
        out_specs=tile(d),
        out_shape=jax.ShapeDtypeStruct((n, d), F32),
        compiler_params=pltpu.CompilerParams(
            dimension_semantics=("parallel",), vmem_limit_bytes=VMEM_LIMIT_BYTES),
        name="merge_ffn",
    )(x2, ys, ya, gates, *consts)


def _lane_bcast(v, n=LANES):
    return jnp.broadcast_to(v.astype(F32)[:, None], (v.shape[0], n))


def _layer(x, meta_chunk, l, norm_mix_g, w_in, gate_bias, conv_w, conv_b, dt_bias, a_log, d_skip,
           ssd_norm_g, lam_vecs, subln_g, w_ssd_branch, w_attn_branch, w_out, norm_ffn_g,
           w_gate_ffn, w_up_ffn, w_down_ffn, norm_final_g):
    nb, seq, d = x.shape
    sizes = [SSD_INNER, SSD_CONV_DIM, SSD_HEADS, ATT_QK, ATT_QK, ATT_V, 2 * d]
    offs = [0]
    for s in sizes:
        offs.append(offs[-1] + s)
    col = lambda idx: w_in[:, offs[idx]:offs[idx + 1]].astype(BF16)
    w = {"z_t": col(0).T, "xbc_t": col(1).T, "dt_t": col(2).T, "q": col(3), "k": col(4),
         "v_t": col(5).T, "gates": col(6)}
    g_mix = norm_mix_g.astype(F32)[None, :]

    q, k, gates, xbct, zt, vt, dtt = _in_projection(
        x, g_mix, w, rows=INPROJ_ROWS, masked_rows=0)
    _, k_meta, _, xbct_meta, _, vt_meta, dtt_meta = _in_projection(
        meta_chunk, g_mix, w, rows=BLOCK, masked_rows=PAD_FRONT)

    ssd_params = [
        jnp.stack([_lane_bcast(conv_w[:, t]) for t in range(SSD_CONV)]),
        _lane_bcast(conv_b), _lane_bcast(dt_bias), _lane_bcast(a_log),
        _lane_bcast(jnp.repeat(d_skip, SSD_HEADDIM)), _lane_bcast(ssd_norm_g)]
    y_ssd = _ssd(xbct, dtt, zt, xbct_meta, dtt_meta, ssd_params)

    lam_init = 0.8 - 0.6 * math.exp(-0.3 * l)
    y_att = _attention(q, k, vt, k_meta, vt_meta, lam_vecs.astype(F32),
                       _lane_bcast(subln_g, ATTN_Q_ROWS), lam_init=lam_init)

    consts = [gate_bias.astype(F32)[None, :], w_ssd_branch.astype(BF16),
              w_attn_branch.astype(BF16), w_out.astype(BF16), norm_ffn_g.astype(F32)[None, :],
              w_gate_ffn.astype(BF16), w_up_ffn.astype(BF16), w_down_ffn.astype(BF16),
              norm_final_g.astype(F32)[None, :]]
    out = _merge_ffn(x.reshape(nb * seq, d), y_ssd.reshape(nb * seq, SSD_INNER),
                     y_att.reshape(nb * seq, ATT_V), gates.reshape(nb * seq, 2 * d), consts)
    return out.reshape(nb, seq, d)


def kernel(x, meta_tokens, norm_mix_g, w_in, gate_bias, conv_w, conv_b, dt_bias, a_log, d_skip, ssd_norm_g, lambda_q1, lambda_k1, lambda_q2, lambda_k2, subln_g, w_ssd_branch, w_attn_branch, w_out, norm_ffn_g, w_gate_ffn, w_up_ffn, w_down_ffn, norm_final_g):
    depth = w_in.shape[0]
    assert depth == 1, "the fused final norm assumes a single layer"
    d = x.shape[-1]
    meta_chunk = jnp.concatenate(
        [jnp.zeros((PAD_FRONT, d), x.dtype), meta_tokens.astype(x.dtype)], axis=0)[None]
    l = 0
    lam_vecs = jnp.stack([lambda_q1[l], lambda_k1[l], lambda_q2[l], lambda_k2[l]])
    return _layer(x, meta_chunk, l, norm_mix_g[l], w_in[l], gate_bias[l], conv_w[l], conv_b[l],
                  dt_bias[l], a_log[l], d_skip[l], ssd_norm_g[l], lam_vecs, subln_g[l],
                  w_ssd_branch[l], w_attn_branch[l], w_out[l], norm_ffn_g[l], w_gate_ffn[l],
                  w_up_ffn[l], w_down_ffn[l], norm_final_g)
```

```python
import functools
import math

import jax
import jax.numpy as jnp
from jax import lax
from jax.experimental import pallas as pl
from jax.experimental.pallas import tpu as pltpu

F32 = jnp.float32
BF16 = jnp.bfloat16

N_META = 16
BLOCK = 128
PAD_FRONT = BLOCK - N_META
NORM_EPS = 1e-6
NEG = -1e30

SSD_HEADDIM = 64
SSD_HEADS = 16
SSD_INNER = SSD_HEADS * SSD_HEADDIM
SSD_GROUPS = 2
SSD_STATE = 128
SSD_CONV = 4
SSD_BC = SSD_GROUPS * SSD_STATE
SSD_CONV_DIM = SSD_INNER + 2 * SSD_BC
HEADS_PER_GROUP = SSD_HEADS // SSD_GROUPS
GROUP_INNER = HEADS_PER_GROUP * SSD_HEADDIM

ATT_HEADS = 8
ATT_HEAD_DIM = 64
ATT_QK = ATT_HEADS * 2 * ATT_HEAD_DIM
ATT_V = ATT_HEADS * 2 * ATT_HEAD_DIM
ATT_HEAD_COLS = 2 * ATT_HEAD_DIM

LOG2E = math.log2(math.e)
Q_SCALE = ATT_HEAD_DIM ** -0.5 * LOG2E

LANES = 128
VMEM_LIMIT_BYTES = 56 * 1024 * 1024
INPROJ_ROWS = 512
CONV_ROWS = 64
ATTN_TILE = 512
ATTN_RUN = 4
POS_RADIX = 16
DENOM_ROWS = 16
FFN_ROWS = 256


def _resident(shape):
    zeros = (0,) * len(shape)
    return pl.BlockSpec(shape, lambda *_: zeros, pipeline_mode=pl.Buffered(1))


def _sigmoid(x):
    return 1.0 / (1.0 + jnp.exp(-x))


def _silu(x):
    return x * _sigmoid(x)


def _dot(a, b):
    return jnp.dot(a, b, preferred_element_type=F32)


def _dot_nt(a, b):
    return lax.dot_general(a, b, (((1,), (1,)), ((), ())), preferred_element_type=F32)


def _dot_tn(a, b):
    return lax.dot_general(a, b, (((0,), (0,)), ((), ())), preferred_element_type=F32)


def _inproj_kernel(x_ref, g_ref, wq_ref, wk_ref, wg_ref, wxbct_ref, wzt_ref, wvt_ref, wdtt_ref,
                   q_ref, k_ref, gates_ref, xbct_ref, zt_ref, vt_ref, dtt_ref, *, masked_rows):
    x = x_ref[0]
    ms = jnp.mean(x * x, axis=-1, keepdims=True)
    u = x * lax.rsqrt(ms + NORM_EPS) * g_ref[...]
    if masked_rows:
        row = lax.broadcasted_iota(jnp.int32, u.shape, 0)
        u = jnp.where(row >= masked_rows, u, 0.0)
    ub = u.astype(BF16)
    q_ref[0] = (_dot(ub, wq_ref[...]) * Q_SCALE).astype(q_ref.dtype)
    k_ref[0] = _dot(ub, wk_ref[...]).astype(k_ref.dtype)
    gates_ref[0] = _dot(ub, wg_ref[...]).astype(gates_ref.dtype)
    xbct_ref[0] = _dot_nt(wxbct_ref[...], ub)
    zt_ref[0] = _dot_nt(wzt_ref[...], ub).astype(zt_ref.dtype)
    vt_ref[0] = _dot_nt(wvt_ref[...], ub).astype(vt_ref.dtype)
    dtt_ref[0] = _dot_nt(wdtt_ref[...], ub)


def _in_projection(h, g, w, *, rows, masked_rows):
    nb, seq, d = h.shape
    grid = (nb, seq // rows)
    row_major = lambda cols, dt: (jax.ShapeDtypeStruct((nb, seq, cols), dt),
                                  pl.BlockSpec((1, rows, cols), lambda b, i: (b, i, 0)))
    chan_major = lambda cols, dt: (jax.ShapeDtypeStruct((nb, cols, seq), dt),
                                   pl.BlockSpec((1, cols, rows), lambda b, i: (b, 0, i)))
    outs = [row_major(ATT_QK, BF16), row_major(ATT_QK, BF16), row_major(2 * d, BF16),
            chan_major(SSD_CONV_DIM, F32), chan_major(SSD_INNER, BF16), chan_major(ATT_V, BF16),
            chan_major(SSD_HEADS, F32)]
    weights = [w["q"], w["k"], w["gates"], w["xbc_t"], w["z_t"], w["v_t"], w["dt_t"]]
    return pl.pallas_call(
        functools.partial(_inproj_kernel, masked_rows=masked_rows),
        grid=grid,
        in_specs=[pl.BlockSpec((1, rows, d), lambda b, i: (b, i, 0)), _resident(g.shape)]
                 + [_resident(wi.shape) for wi in weights],
        out_specs=[o[1] for o in outs],
        out_shape=[o[0] for o in outs],
        compiler_params=pltpu.CompilerParams(
            dimension_semantics=("parallel", "parallel"), vmem_limit_bytes=VMEM_LIMIT_BYTES),
        name="in_projection",
    )(h, g, *weights)


def _ssd_chunk(x_ref, dt_raw, zt_ref, y_ref, convw_ref, convb_ref, dtb_ref, alog_ref, dskip_ref,
               ng_ref, state_ref, prev_ref, xbc_ref, ygate_ref, *, first):
    lane = lax.broadcasted_iota(jnp.int32, (CONV_ROWS, BLOCK), 1)
    for blk in range(SSD_CONV_DIM // CONV_ROWS):
        rs = slice(blk * CONV_ROWS, (blk + 1) * CONV_ROWS)
        x = x_ref[0, rs, :]
        x_prev = jnp.zeros_like(x) if first else prev_ref[rs, :]
        acc = convb_ref[rs, :] + convw_ref[SSD_CONV - 1, rs, :] * x
        for j in range(1, SSD_CONV):
            joined = jnp.where(lane >= BLOCK - j, x_prev, x)
            acc = acc + convw_ref[SSD_CONV - 1 - j, rs, :] * pltpu.roll(joined, j, 1)
        xbc_ref[rs, :] = _silu(acc)
        prev_ref[rs, :] = x

    v = dt_raw + dtb_ref[...]
    dt = jnp.maximum(v, 0.0) + jnp.log1p(jnp.exp(-jnp.abs(v)))
    if first:
        step = lax.broadcasted_iota(jnp.int32, dt.shape, 1)
        dt = jnp.where(step >= PAD_FRONT, dt, 0.0)
    da = dt * (-jnp.exp(alog_ref[...]))
    r = lax.broadcasted_iota(jnp.int32, (BLOCK, BLOCK), 0)
    c = lax.broadcasted_iota(jnp.int32, (BLOCK, BLOCK), 1)
    upper = r <= c
    acs = jnp.dot(da, upper.astype(F32), preferred_element_type=F32,
                  precision=lax.Precision.HIGHEST)
    last = acs[:, BLOCK - 1:BLOCK]
    to_end = jnp.exp(last - acs) * dt
    chunk_decay = jnp.broadcast_to(jnp.exp(last), acs.shape)
    if not first:
        acs_col = jnp.concatenate([acs, jnp.zeros((BLOCK - SSD_HEADS, BLOCK), F32)], axis=0).T
        grow = jnp.exp(acs)
        ssq = jnp.zeros((1, BLOCK), F32)

    for g in range(SSD_GROUPS):
        b_rows = slice(SSD_INNER + g * SSD_STATE, SSD_INNER + (g + 1) * SSD_STATE)
        c_rows = slice(SSD_INNER + SSD_BC + g * SSD_STATE, SSD_INNER + SSD_BC + (g + 1) * SSD_STATE)
        bg = xbc_ref[b_rows, :].astype(BF16)
        if not first:
            cg = xbc_ref[c_rows, :].astype(BF16)
            cb_t = _dot_tn(bg, cg)
        for e in range(HEADS_PER_GROUP):
            h = g * HEADS_PER_GROUP + e
            rows = slice(h * SSD_HEADDIM, (h + 1) * SSD_HEADDIM)
            xs = xbc_ref[rows, :]
            inject = _dot_nt((xs * to_end[h:h + 1, :]).astype(BF16), bg)
            if first:
                state_ref[rows, :] = inject
                continue
            st = state_ref[rows, :]
            state_ref[rows, :] = st * chunk_decay[h:h + 1, :] + inject
            seg = acs[h:h + 1, :] - acs_col[:, h:h + 1]
            g_t = (cb_t * jnp.where(upper, jnp.exp(seg), 0.0)).astype(BF16)
            y = _dot((xs * dt[h:h + 1, :]).astype(BF16), g_t)
            y = y + _dot(st.astype(BF16), cg) * grow[h:h + 1, :] + xs * dskip_ref[rows, :]
            y = y * _silu(zt_ref[0, rows, :].astype(F32))
            ssq = ssq + jnp.sum(y * y, axis=0, keepdims=True)
            ygate_ref[rows, :] = y

    if not first:
        scale = lax.rsqrt(ssq * (1.0 / SSD_INNER) + NORM_EPS)
        for blk in range(SSD_INNER // BLOCK):
            rs = slice(blk * BLOCK, (blk + 1) * BLOCK)
            y = ygate_ref[rs, :] * scale * ng_ref[rs, :]
            y_ref[0, :, rs] = y.T.astype(y_ref.dtype)


def _ssd_kernel(xbct_ref, dtt_ref, zt_ref, xbct_meta_ref, dtt_meta_ref, convw_ref, convb_ref,
                dtb_ref, alog_ref, dskip_ref, ng_ref, y_ref, state_ref, prev_ref, xbc_ref,
                ygate_ref):
    chunk = functools.partial(
        _ssd_chunk, zt_ref=zt_ref, y_ref=y_ref, convw_ref=convw_ref, convb_ref=convb_ref,
        dtb_ref=dtb_ref, alog_ref=alog_ref, dskip_ref=dskip_ref, ng_ref=ng_ref,
        state_ref=state_ref, prev_ref=prev_ref, xbc_ref=xbc_ref, ygate_ref=ygate_ref)

    @pl.when(pl.program_id(1) == 0)
    def _():
        chunk(xbct_meta_ref, dtt_meta_ref[0], first=True)

    chunk(xbct_ref, dtt_ref[0], first=False)


def _ssd(xbct, dtt, zt, xbct_meta, dtt_meta, params):
    nb, _, seq = xbct.shape
    chunk = lambda rows: pl.BlockSpec((1, rows, BLOCK), lambda b, c: (b, 0, c))
    return pl.pallas_call(
        _ssd_kernel,
        grid=(nb, seq // BLOCK),
        in_specs=[chunk(SSD_CONV_DIM), chunk(SSD_HEADS), chunk(SSD_INNER),
                  _resident(xbct_meta.shape), _resident(dtt_meta.shape)]
                 + [_resident(a.shape) for a in params],
        out_specs=pl.BlockSpec((1, BLOCK, SSD_INNER), lambda b, c: (b, c, 0)),
        out_shape=jax.ShapeDtypeStruct((nb, seq, SSD_INNER), BF16),
        scratch_shapes=[pltpu.VMEM((SSD_INNER, SSD_STATE), F32),
                        pltpu.VMEM((SSD_CONV_DIM, BLOCK), F32),
                        pltpu.VMEM((SSD_CONV_DIM, BLOCK), F32),
                        pltpu.VMEM((SSD_INNER, BLOCK), F32)],
        compiler_params=pltpu.CompilerParams(
            dimension_semantics=("parallel", "arbitrary"), vmem_limit_bytes=VMEM_LIMIT_BYTES),
        name="ssd",
    )(xbct, dtt, zt, xbct_meta, dtt_meta, *params)


def _attn_kernel(q_ref, k_ref, vt_ref, k_meta_ref, vt_meta_ref, slope_ref, coef_ref, pos_ref,
                 lam_ref, g_ref, o_ref, qa_ref, m_ref, acc_ref, sa_ref, ca_ref, sb_ref, cb_ref, *,
                 lam_init, n_tiles):
    j = pl.program_id(2)
    t = ATTN_TILE
    hc = ATT_HEAD_COLS
    q_tiles = (j, n_tiles - 1 - j)
    slope = slope_ref[0][:, 0:1]
    key_row = lax.broadcasted_iota(jnp.int32, (t, t), 0)
    qry_col = lax.broadcasted_iota(jnp.int32, (t, t), 1)
    meta_row = lax.broadcasted_iota(jnp.int32, (BLOCK, t), 0)

    def rows(tile):
        return pl.ds(pl.multiple_of(tile * t, t), t)

    coef = jnp.broadcast_to(coef_ref[0], (t, LANES)).astype(BF16)
    lane = lax.broadcasted_iota(jnp.int32, (t, hc), 1)
    for w, tile in enumerate(q_tiles):
        qf = q_ref[0, rows(tile), :]
        for comp, keep in enumerate((lane < ATT_HEAD_DIM, lane >= ATT_HEAD_DIM)):
            qa_ref[w, comp] = jnp.concatenate([jnp.where(keep, qf, jnp.zeros_like(qf)), coef],
                                              axis=1)
    m_ref[...] = jnp.full(m_ref.shape, NEG, F32)
    acc_ref[...] = jnp.zeros(acc_ref.shape, F32)

    def scores(kt, w, mask):
        k_aug = jnp.concatenate([kt, pos_ref[0:kt.shape[0], :]], axis=1)
        out = []
        for comp in range(2):
            s = _dot_nt(k_aug, qa_ref[w, comp])
            out.append(s if mask is None else jnp.where(mask, s, NEG))
        return out

    def fold(w, s_of, cmax_of, vtt, key_start):
        sigma = slope * lax.convert_element_type(key_start, F32)
        v_aug = jnp.concatenate([vtt, jnp.ones((DENOM_ROWS, vtt.shape[1]), BF16)], axis=0)
        for comp in range(2):
            m_old = m_ref[w, comp]
            m_new = jnp.maximum(m_old, cmax_of(comp) + sigma)
            alpha = jnp.exp2(m_old - m_new)
            pr = jnp.exp2(s_of(comp) - (m_new - sigma))
            acc_ref[w, comp] = alpha * acc_ref[w, comp] + _dot(v_aug, pr.astype(BF16))
            m_ref[w, comp] = m_new

    n_items = n_tiles + 1
    assert (n_items - 1) % ATTN_RUN == 0 and ATTN_RUN % 2 == 0

    def item(r):
        r = jnp.asarray(r, jnp.int32)
        is_first, is_last = r == 0, r == n_items - 1
        w_full = (r - 1 >= j).astype(jnp.int32)
        w = jnp.where(is_last, 1, jnp.where(is_first, 0, w_full))
        key_tile = jnp.where(is_first, q_tiles[0],
                             jnp.where(is_last, q_tiles[1], r - 1 - w_full * j))
        return w, key_tile, jnp.where(w == 0, q_tiles[0], q_tiles[1])

    slots = ((sa_ref, ca_ref), (sb_ref, cb_ref))

    def stage(r, parity, masked):
        w, key_tile, _ = item(r)
        stage_ref, cmax_ref = slots[parity]
        mask = (key_row <= qry_col) if masked else None
        for comp, s in enumerate(scores(k_ref[0, rows(key_tile), :], w, mask)):
            stage_ref[comp] = s
            cmax_ref[comp] = jnp.max(s, axis=0, keepdims=True)

    def fold_staged(r, parity):
        w, key_tile, q_tile = item(r)
        stage_ref, cmax_ref = slots[parity]
        fold(w, lambda comp: stage_ref[comp], lambda comp: cmax_ref[comp],
             vt_ref[0, :, rows(key_tile)], (key_tile - q_tile) * t)

    stage(0, 0, True)
    for w, tile in enumerate(q_tiles):
        s_meta = scores(k_meta_ref[0], w, meta_row >= PAD_FRONT)
        fold(w, lambda comp: s_meta[comp],
             lambda comp: jnp.max(s_meta[comp], axis=0, keepdims=True),
             vt_meta_ref[0], -(BLOCK + tile * t))

    def run(base, ends_on_diagonal):
        for c in range(1, ATTN_RUN + 1):
            stage(base + c, c % 2, ends_on_diagonal and c == ATTN_RUN)
            fold_staged(base + c - 1, (c - 1) % 2)

    def run_body(u, carry):
        run(u * ATTN_RUN, False)
        return carry

    n_runs = (n_items - 1) // ATTN_RUN
    lax.fori_loop(0, n_runs - 1, run_body, 0)
    run((n_runs - 1) * ATTN_RUN, True)
    fold_staged(n_items - 1, (n_items - 1) % 2)

    lam = (jnp.exp(jnp.sum(lam_ref[0:1] * lam_ref[1:2], axis=-1, keepdims=True))
           - jnp.exp(jnp.sum(lam_ref[2:3] * lam_ref[3:4], axis=-1, keepdims=True)) + lam_init)
    for w, tile in enumerate(q_tiles):
        out = [acc_ref[w, comp, 0:hc, :] * (1.0 / acc_ref[w, comp, hc:hc + 1, :])
               for comp in range(2)]
        o = out[0] - lam * out[1]
        ms = jnp.mean(o * o, axis=0, keepdims=True)
        y = o * lax.rsqrt(ms + NORM_EPS) * g_ref[...] * (1.0 - lam_init)
        o_ref[0, rows(tile), :] = y.T.astype(o_ref.dtype)


def _alibi_operands():
    heads = jnp.arange(ATT_HEADS, dtype=F32)
    slope = 2.0 ** (-8.0 * (heads + 1.0) / ATT_HEADS) * LOG2E
    pieces, rest = [], slope
    for _ in range(3):
        piece = rest.astype(BF16).astype(F32)
        pieces.append(piece)
        rest = rest - piece
    cols = [POS_RADIX * p for p in pieces] + pieces
    coef = jnp.zeros((ATT_HEADS, 1, LANES), F32)
    coef = coef.at[:, 0, :len(cols)].set(jnp.stack(cols, axis=1))
    row = jnp.arange(ATTN_TILE)
    pos = jnp.zeros((ATTN_TILE, LANES), F32)
    pos = pos.at[:, 0:3].set((row // POS_RADIX).astype(F32)[:, None])
    pos = pos.at[:, 3:6].set((row % POS_RADIX).astype(F32)[:, None])
    slope = jnp.broadcast_to(slope[:, None, None], (ATT_HEADS, 1, LANES))
    return slope, coef, pos.astype(BF16)


def _attention(q, k, vt, k_meta, vt_meta, lam_vecs, subln_g, *, lam_init):
    nb, seq, _ = q.shape
    t = ATTN_TILE
    hc = ATT_HEAD_COLS
    n_tiles = seq // t
    assert n_tiles % 2 == 0
    slopes, coef, pos = _alibi_operands()
    per_head_rows = pl.BlockSpec((1, seq, hc), lambda b, h, j: (b, 0, h))
    return pl.pallas_call(
        functools.partial(_attn_kernel, lam_init=lam_init, n_tiles=n_tiles),
        grid=(nb, ATT_HEADS, n_tiles // 2),
        in_specs=[per_head_rows, per_head_rows,
                  pl.BlockSpec((1, hc, seq), lambda b, h, j: (b, h, 0)),
                  pl.BlockSpec((1, BLOCK, hc), lambda b, h, j: (0, 0, h)),
                  pl.BlockSpec((1, hc, BLOCK), lambda b, h, j: (0, h, 0)),
                  pl.BlockSpec((1, 1, LANES), lambda b, h, j: (h, 0, 0)),
                  pl.BlockSpec((1, 1, LANES), lambda b, h, j: (h, 0, 0)),
                  _resident(pos.shape), _resident(lam_vecs.shape), _resident(subln_g.shape)],
        out_specs=per_head_rows,
        out_shape=jax.ShapeDtypeStruct((nb, seq, ATT_V), BF16),
        scratch_shapes=[pltpu.VMEM((2, 2, t, 2 * LANES), BF16), pltpu.VMEM((2, 2, 1, t), F32),
                        pltpu.VMEM((2, 2, hc + DENOM_ROWS, t), F32)]
                       + [pltpu.VMEM((2, t, t), F32), pltpu.VMEM((2, 1, t), F32)] * 2,
        compiler_params=pltpu.CompilerParams(
            dimension_semantics=("parallel", "parallel", "arbitrary"),
            vmem_limit_bytes=VMEM_LIMIT_BYTES),
        name="diff_attention",
    )(q, k, vt, k_meta, vt_meta, slopes, coef, pos, lam_vecs, subln_g)


def _merge_ffn_kernel(x_ref, ys_ref, ya_ref, gates_ref, gbias_ref, ws_ref, wa_ref, wo_ref,
                      nffn_ref, wg_ref, wu_ref, wd_ref, nfin_ref, o_ref):
    d = x_ref.shape[-1]
    gates = _sigmoid(gates_ref[...].astype(F32) + gbias_ref[...])
    merged = (gates[:, :d] * _dot(ys_ref[...], ws_ref[...])
              + gates[:, d:] * _dot(ya_ref[...], wa_ref[...]))
    h = x_ref[...] + _dot(merged.astype(BF16), wo_ref[...])
    u = h * lax.rsqrt(jnp.mean(h * h, axis=-1, keepdims=True) + NORM_EPS) * nffn_ref[...]
    ub = u.astype(BF16)
    hidden = _silu(_dot(ub, wg_ref[...])) * _dot(ub, wu_ref[...])
    h = h + _dot(hidden.astype(BF16), wd_ref[...])
    o_ref[...] = h * lax.rsqrt(jnp.mean(h * h, axis=-1, keepdims=True) + NORM_EPS) * nfin_ref[...]


def _merge_ffn(x2, ys, ya, gates, consts):
    n, d = x2.shape
    rows = FFN_ROWS
    tile = lambda cols: pl.BlockSpec((rows, cols), lambda i: (i, 0))
    gbias, ws, wa, wo, nffn, wg, wu, wd, nfin = consts
    return pl.pallas_call(
        _merge_ffn_kernel,
        grid=(n // rows,),
        in_specs=[tile(d), tile(ys.shape[1]), tile(ya.shape[1]), tile(gates.shape[1])]
                 + [_resident(a.shape) for a in consts],
        out_specs=tile(d),
        out_shape=jax.ShapeDtypeStruct((n, d), F32),
        compiler_params=pltpu.CompilerParams(
            dimension_semantics=("parallel",), vmem_limit_bytes=VMEM_LIMIT_BYTES),
        name="merge_ffn",
    )(x2, ys, ya, gates, *consts)


def _lane_bcast(v, n=LANES):
    return jnp.broadcast_to(v.astype(F32)[:, None], (v.shape[0], n))


def _layer(x, meta_chunk, l, norm_mix_g, w_in, gate_bias, conv_w, conv_b, dt_bias, a_log, d_skip,
           ssd_norm_g, lam_vecs, subln_g, w_ssd_branch, w_attn_branch, w_out, norm_ffn_g,
           w_gate_ffn, w_up_ffn, w_down_ffn, norm_final_g):
    nb, seq, d = x.shape
    sizes = [SSD_INNER, SSD_CONV_DIM, SSD_HEADS, ATT_QK, ATT_QK, ATT_V, 2 * d]
    offs = [0]
    for s in sizes:
        offs.append(offs[-1] + s)
    col = lambda idx: w_in[:, offs[idx]:offs[idx + 1]].astype(BF16)
    w = {"z_t": col(0).T, "xbc_t": col(1).T, "dt_t": col(2).T, "q": col(3), "k": col(4),
         "v_t": col(5).T, "gates": col(6)}
    g_mix = norm_mix_g.astype(F32)[None, :]

    q, k, gates, xbct, zt, vt, dtt = _in_projection(
        x, g_mix, w, rows=INPROJ_ROWS, masked_rows=0)
    _, k_meta, _, xbct_meta, _, vt_meta, dtt_meta = _in_projection(
        meta_chunk, g_mix, w, rows=BLOCK, masked_rows=PAD_FRONT)

    ssd_params = [
        jnp.stack([_lane_bcast(conv_w[:, t]) for t in range(SSD_CONV)]),
        _lane_bcast(conv_b), _lane_bcast(dt_bias), _lane_bcast(a_log),
        _lane_bcast(jnp.repeat(d_skip, SSD_HEADDIM)), _lane_bcast(ssd_norm_g)]
    y_ssd = _ssd(xbct, dtt, zt, xbct_meta, dtt_meta, ssd_params)

    lam_init = 0.8 - 0.6 * math.exp(-0.3 * l)
    y_att = _attention(q, k, vt, k_meta, vt_meta, lam_vecs.astype(F32),
                       _lane_bcast(subln_g, ATTN_TILE), lam_init=lam_init)

    consts = [gate_bias.astype(F32)[None, :], w_ssd_branch.astype(BF16),
              w_attn_branch.astype(BF16), w_out.astype(BF16), norm_ffn_g.astype(F32)[None, :],
              w_gate_ffn.astype(BF16), w_up_ffn.astype(BF16), w_down_ffn.astype(BF16),
              norm_final_g.astype(F32)[None, :]]
    out = _merge_ffn(x.reshape(nb * seq, d), y_ssd.reshape(nb * seq, SSD_INNER),
                     y_att.reshape(nb * seq, ATT_V), gates.reshape(nb * seq, 2 * d), consts)
    return out.reshape(nb, seq, d)


def kernel(x, meta_tokens, norm_mix_g, w_in, gate_bias, conv_w, conv_b, dt_bias, a_log, d_skip, ssd_norm_g, lambda_q1, lambda_k1, lambda_q2, lambda_k2, subln_g, w_ssd_branch, w_attn_branch, w_out, norm_ffn_g, w_gate_ffn, w_up_ffn, w_down_ffn, norm_final_g):
    depth = w_in.shape[0]
    assert depth == 1, "the fused final norm assumes a single layer"
    d = x.shape[-1]
    meta_chunk = jnp.concatenate(
        [jnp.zeros((PAD_FRONT, d), x.dtype), meta_tokens.astype(x.dtype)], axis=0)[None]
    l = 0
    lam_vecs = jnp.stack([lambda_q1[l], lambda_k1[l], lambda_q2[l], lambda_k2[l]])
    return _layer(x, meta_chunk, l, norm_mix_g[l], w_in[l], gate_bias[l], conv_w[l], conv_b[l],
                  dt_bias[l], a_log[l], d_skip[l], ssd_norm_g[l], lam_vecs, subln_g[l],
                  w_ssd_branch[l], w_attn_branch[l], w_out[l], norm_ffn_g[l], w_gate_ffn[l],
                  w_up_ffn[l], w_down_ffn[l], norm_final_g)
```

```python
import functools
import math

import jax
import jax.numpy as jnp
import numpy as np
from jax import lax
from jax.experimental import pallas as pl
from jax.experimental.pallas import tpu as pltpu

F32 = jnp.float32
BF16 = jnp.bfloat16

N_META = 16
BLOCK = 128
PAD_FRONT = BLOCK - N_META
NORM_EPS = 1e-6
NEG = -1e30

SSD_HEADDIM = 64
SSD_HEADS = 16
SSD_INNER = SSD_HEADS * SSD_HEADDIM
SSD_GROUPS = 2
SSD_STATE = 128
SSD_CONV = 4
SSD_BC = SSD_GROUPS * SSD_STATE
SSD_CONV_DIM = SSD_INNER + 2 * SSD_BC
HEADS_PER_GROUP = SSD_HEADS // SSD_GROUPS
GROUP_INNER = HEADS_PER_GROUP * SSD_HEADDIM

ATT_HEADS = 8
ATT_HEAD_DIM = 64
ATT_QK = ATT_HEADS * 2 * ATT_HEAD_DIM
ATT_V = ATT_HEADS * 2 * ATT_HEAD_DIM
ATT_HEAD_COLS = 2 * ATT_HEAD_DIM

LOG2E = math.log2(math.e)
Q_SCALE = ATT_HEAD_DIM ** -0.5 * LOG2E

LANES = 128
VMEM_LIMIT_BYTES = 56 * 1024 * 1024
INPROJ_ROWS = 512
CONV_ROWS = 64
ATTN_TILE = 512
ATTN_RUN = 4
POS_RADIX = 16
DENOM_ROWS = 16
FFN_ROWS = 256


def _resident(shape):
    zeros = (0,) * len(shape)
    return pl.BlockSpec(shape, lambda *_: zeros, pipeline_mode=pl.Buffered(1))


def _sigmoid(x):
    return 1.0 / (1.0 + jnp.exp2(x * (-LOG2E)))


def _silu(x):
    return x * _sigmoid(x)


def _dot(a, b):
    return jnp.dot(a, b, preferred_element_type=F32)


def _dot_nt(a, b):
    return lax.dot_general(a, b, (((1,), (1,)), ((), ())), preferred_element_type=F32)


def _dot_tn(a, b):
    return lax.dot_general(a, b, (((0,), (0,)), ((), ())), preferred_element_type=F32)


def _inproj_kernel(x_ref, g_ref, wq_ref, wk_ref, wg_ref, wxbct_ref, wzt_ref, wvt_ref, wdtt_ref,
                   convw_ref, convb_ref, hist_ref, q_ref, k_ref, gates_ref, xbct_ref, zt_ref,
                   vt_ref, dtt_ref, tail_ref, raw_ref, *, masked_rows):
    rows = x_ref.shape[1]
    x = x_ref[0]
    ms = jnp.mean(x * x, axis=-1, keepdims=True)
    u = x * lax.rsqrt(ms + NORM_EPS) * g_ref[...]
    if masked_rows:
        row = lax.broadcasted_iota(jnp.int32, u.shape, 0)
        u = jnp.where(row >= masked_rows, u, 0.0)
    ub = u.astype(BF16)

    @pl.when(pl.program_id(1) == 0)
    def _():
        raw_ref[:, 0:BLOCK] = hist_ref[0]

    raw_ref[:, BLOCK:] = _dot_nt(wxbct_ref[...], ub)
    q_ref[0] = (_dot(ub, wq_ref[...]) * Q_SCALE).astype(q_ref.dtype)
    k_ref[0] = _dot(ub, wk_ref[...]).astype(k_ref.dtype)
    gates_ref[0] = _dot(ub, wg_ref[...]).astype(gates_ref.dtype)
    zt_ref[0] = _dot_nt(wzt_ref[...], ub).astype(zt_ref.dtype)
    vt_ref[0] = _dot_nt(wvt_ref[...], ub).astype(vt_ref.dtype)
    dtt_ref[0] = _dot_nt(wdtt_ref[...], ub)
    lane = lax.broadcasted_iota(jnp.int32, (CONV_ROWS, BLOCK), 1)
    for blk in range(SSD_CONV_DIM // CONV_ROWS):
        rs = slice(blk * CONV_ROWS, (blk + 1) * CONV_ROWS)
        taps = [convw_ref[t, rs, :] for t in range(SSD_CONV)]
        bias = convb_ref[rs, :]
        for cb in range(rows // BLOCK):
            x_prev = raw_ref[rs, cb * BLOCK:(cb + 1) * BLOCK]
            xc = raw_ref[rs, (cb + 1) * BLOCK:(cb + 2) * BLOCK]
            acc = bias + taps[SSD_CONV - 1] * xc
            for j in range(1, SSD_CONV):
                joined = jnp.where(lane >= BLOCK - j, x_prev, xc)
                acc = acc + taps[SSD_CONV - 1 - j] * pltpu.roll(joined, j, 1)
            xbct_ref[0, rs, cb * BLOCK:(cb + 1) * BLOCK] = _silu(acc)
        tail = raw_ref[rs, rows:rows + BLOCK]
        raw_ref[rs, 0:BLOCK] = tail
        tail_ref[0, rs, :] = tail


def _in_projection(h, g, w, conv_w, conv_b, hist, *, rows, masked_rows):
    nb, seq, d = h.shape
    grid = (nb, seq // rows)
    row_major = lambda cols, dt: (jax.ShapeDtypeStruct((nb, seq, cols), dt),
                                  pl.BlockSpec((1, rows, cols), lambda b, i: (b, i, 0)))
    chan_major = lambda cols, dt: (jax.ShapeDtypeStruct((nb, cols, seq), dt),
                                   pl.BlockSpec((1, cols, rows), lambda b, i: (b, 0, i)))
    tail = (jax.ShapeDtypeStruct((nb, SSD_CONV_DIM, BLOCK), F32),
            pl.BlockSpec((1, SSD_CONV_DIM, BLOCK), lambda b, i: (b, 0, 0)))
    outs = [row_major(ATT_QK, BF16), row_major(ATT_QK, BF16), row_major(2 * d, BF16),
            chan_major(SSD_CONV_DIM, F32), chan_major(SSD_INNER, BF16), chan_major(ATT_V, BF16),
            chan_major(SSD_HEADS, F32), tail]
    consts = [g, w["q"], w["k"], w["gates"], w["xbc_t"], w["z_t"], w["v_t"], w["dt_t"],
              conv_w, conv_b, hist]
    return pl.pallas_call(
        functools.partial(_inproj_kernel, masked_rows=masked_rows),
        grid=grid,
        in_specs=[pl.BlockSpec((1, rows, d), lambda b, i: (b, i, 0))]
                 + [_resident(a.shape) for a in consts],
        out_specs=[o[1] for o in outs],
        out_shape=[o[0] for o in outs],
        scratch_shapes=[pltpu.VMEM((SSD_CONV_DIM, BLOCK + rows), F32)],
        compiler_params=pltpu.CompilerParams(
            dimension_semantics=("parallel", "arbitrary"), vmem_limit_bytes=VMEM_LIMIT_BYTES),
        name="in_projection",
    )(h, *consts)


def _ssd_scan(xbc_ref, dt_raw, zt_ref, y_ref, dtb_ref, alog_ref, dskip_ref, ng_ref, state_ref,
              ygate_ref, *, first):
    v = dt_raw + dtb_ref[...]
    dt = jnp.maximum(v, 0.0) + jnp.log1p(jnp.exp(-jnp.abs(v)))
    if first:
        step = lax.broadcasted_iota(jnp.int32, dt.shape, 1)
        dt = jnp.where(step >= PAD_FRONT, dt, 0.0)
    da = dt * (-jnp.exp(alog_ref[...]))
    r = lax.broadcasted_iota(jnp.int32, (BLOCK, BLOCK), 0)
    c = lax.broadcasted_iota(jnp.int32, (BLOCK, BLOCK), 1)
    upper = r <= c
    acs = jnp.dot(da, upper.astype(F32), preferred_element_type=F32,
                  precision=lax.Precision.HIGHEST)
    last = acs[:, BLOCK - 1:BLOCK]
    to_end = jnp.exp(last - acs) * dt
    chunk_decay = jnp.broadcast_to(jnp.exp(last), acs.shape)
    if not first:
        acs_col = jnp.concatenate([acs, jnp.zeros((BLOCK - SSD_HEADS, BLOCK), F32)], axis=0).T
        grow = jnp.exp(acs)
        ssq = jnp.zeros((1, BLOCK), F32)

    for g in range(SSD_GROUPS):
        b_rows = slice(SSD_INNER + g * SSD_STATE, SSD_INNER + (g + 1) * SSD_STATE)
        c_rows = slice(SSD_INNER + SSD_BC + g * SSD_STATE, SSD_INNER + SSD_BC + (g + 1) * SSD_STATE)
        bg = xbc_ref[0, b_rows, :].astype(BF16)
        if not first:
            cg = xbc_ref[0, c_rows, :].astype(BF16)
            cb_t = _dot_tn(bg, cg)
        for e in range(HEADS_PER_GROUP):
            h = g * HEADS_PER_GROUP + e
            rows = slice(h * SSD_HEADDIM, (h + 1) * SSD_HEADDIM)
            xs = xbc_ref[0, rows, :]
            inject = _dot_nt((xs * to_end[h:h + 1, :]).astype(BF16), bg)
            if first:
                state_ref[rows, :] = inject
                continue
            st = state_ref[rows, :]
            state_ref[rows, :] = st * chunk_decay[h:h + 1, :] + inject
            seg = acs[h:h + 1, :] - acs_col[:, h:h + 1]
            g_t = (cb_t * jnp.where(upper, jnp.exp(seg), 0.0)).astype(BF16)
            y = _dot((xs * dt[h:h + 1, :]).astype(BF16), g_t)
            y = y + _dot(st.astype(BF16), cg) * grow[h:h + 1, :] + xs * dskip_ref[rows, :]
            y = y * _silu(zt_ref[0, rows, :].astype(F32))
            ssq = ssq + jnp.sum(y * y, axis=0, keepdims=True)
            ygate_ref[rows, :] = y

    if not first:
        scale = lax.rsqrt(ssq * (1.0 / SSD_INNER) + NORM_EPS)
        for blk in range(SSD_INNER // BLOCK):
            rs = slice(blk * BLOCK, (blk + 1) * BLOCK)
            y = ygate_ref[rs, :] * scale * ng_ref[rs, :]
            y_ref[0, :, rs] = y.T.astype(y_ref.dtype)


def _mixer_kernel(q_ref, k_ref, vt_ref, k_meta_ref, vt_meta_ref, slope_ref, coef_ref, pos_ref,
                  lam_ref, g_ref, xbct_ref, dtt_ref, zt_ref, xbct_meta_ref, dtt_meta_ref,
                  dtb_ref, alog_ref, dskip_ref, ng_ref, o_ref, y_ref, qa_ref, m_ref, acc_ref,
                  sa_ref, ca_ref, sb_ref, cb_ref, state_ref, ygate_ref, *, lam_init, n_tiles):
    j = pl.program_id(2)
    scan = functools.partial(_ssd_scan, zt_ref=zt_ref, y_ref=y_ref, dtb_ref=dtb_ref,
                             alog_ref=alog_ref, dskip_ref=dskip_ref, ng_ref=ng_ref,
                             state_ref=state_ref, ygate_ref=ygate_ref)

    @pl.when((pl.program_id(1) == 0) & (j == 0))
    def _():
        scan(xbct_meta_ref, dtt_meta_ref[0], first=True)

    t = ATTN_TILE
    hc = ATT_HEAD_COLS
    q_tiles = (j, n_tiles - 1 - j)
    slope = slope_ref[0][:, 0:1]
    key_row = lax.broadcasted_iota(jnp.int32, (t, t), 0)
    qry_col = lax.broadcasted_iota(jnp.int32, (t, t), 1)
    meta_row = lax.broadcasted_iota(jnp.int32, (BLOCK, t), 0)

    def rows(tile):
        return pl.ds(pl.multiple_of(tile * t, t), t)

    coef = jnp.broadcast_to(coef_ref[0], (t, LANES)).astype(BF16)
    lane = lax.broadcasted_iota(jnp.int32, (t, hc), 1)
    for w, tile in enumerate(q_tiles):
        qf = q_ref[0, rows(tile), :]
        for comp, keep in enumerate((lane < ATT_HEAD_DIM, lane >= ATT_HEAD_DIM)):
            qa_ref[w, comp] = jnp.concatenate([jnp.where(keep, qf, jnp.zeros_like(qf)), coef],
                                              axis=1)
    m_ref[...] = jnp.full(m_ref.shape, NEG, F32)
    acc_ref[...] = jnp.zeros(acc_ref.shape, F32)

    def scores(kt, w, mask):
        k_aug = jnp.concatenate([kt, pos_ref[0:kt.shape[0], :]], axis=1)
        out = []
        for comp in range(2):
            s = _dot_nt(k_aug, qa_ref[w, comp])
            out.append(s if mask is None else jnp.where(mask, s, NEG))
        return out

    def fold(w, s_of, cmax_of, vtt, key_start):
        sigma = slope * lax.convert_element_type(key_start, F32)
        v_aug = jnp.concatenate([vtt, jnp.ones((DENOM_ROWS, vtt.shape[1]), BF16)], axis=0)
        for comp in range(2):
            m_old = m_ref[w, comp]
            m_new = jnp.maximum(m_old, cmax_of(comp) + sigma)
            alpha = jnp.exp2(m_old - m_new)
            pr = jnp.exp2(s_of(comp) - (m_new - sigma))
            acc_ref[w, comp] = alpha * acc_ref[w, comp] + _dot(v_aug, pr.astype(BF16))
            m_ref[w, comp] = m_new

    n_items = n_tiles + 1
    assert (n_items - 1) % ATTN_RUN == 0 and ATTN_RUN % 2 == 0

    def item(r):
        r = jnp.asarray(r, jnp.int32)
        is_first, is_last = r == 0, r == n_items - 1
        w_full = (r - 1 >= j).astype(jnp.int32)
        w = jnp.where(is_last, 1, jnp.where(is_first, 0, w_full))
        key_tile = jnp.where(is_first, q_tiles[0],
                             jnp.where(is_last, q_tiles[1], r - 1 - w_full * j))
        return w, key_tile, jnp.where(w == 0, q_tiles[0], q_tiles[1])

    slots = ((sa_ref, ca_ref), (sb_ref, cb_ref))

    def stage(r, parity, masked):
        w, key_tile, _ = item(r)
        stage_ref, cmax_ref = slots[parity]
        mask = (key_row <= qry_col) if masked else None
        for comp, s in enumerate(scores(k_ref[0, rows(key_tile), :], w, mask)):
            stage_ref[comp] = s
            cmax_ref[comp] = jnp.max(s, axis=0, keepdims=True)

    def fold_staged(r, parity):
        w, key_tile, q_tile = item(r)
        stage_ref, cmax_ref = slots[parity]
        fold(w, lambda comp: stage_ref[comp], lambda comp: cmax_ref[comp],
             vt_ref[0, :, rows(key_tile)], (key_tile - q_tile) * t)

    stage(0, 0, True)
    for w, tile in enumerate(q_tiles):
        s_meta = scores(k_meta_ref[0], w, meta_row >= PAD_FRONT)
        fold(w, lambda comp: s_meta[comp],
             lambda comp: jnp.max(s_meta[comp], axis=0, keepdims=True),
             vt_meta_ref[0], -(BLOCK + tile * t))

    def run(base, ends_on_diagonal):
        for c in range(1, ATTN_RUN + 1):
            stage(base + c, c % 2, ends_on_diagonal and c == ATTN_RUN)
            fold_staged(base + c - 1, (c - 1) % 2)

    def run_body(u, carry):
        run(u * ATTN_RUN, False)
        return carry

    n_runs = (n_items - 1) // ATTN_RUN
    lax.fori_loop(0, n_runs - 1, run_body, 0)
    scan(xbct_ref, dtt_ref[0], first=False)
    run((n_runs - 1) * ATTN_RUN, True)
    fold_staged(n_items - 1, (n_items - 1) % 2)

    lam = (jnp.exp(jnp.sum(lam_ref[0:1] * lam_ref[1:2], axis=-1, keepdims=True))
           - jnp.exp(jnp.sum(lam_ref[2:3] * lam_ref[3:4], axis=-1, keepdims=True)) + lam_init)
    for w, tile in enumerate(q_tiles):
        out = [acc_ref[w, comp, 0:hc, :] * (1.0 / acc_ref[w, comp, hc:hc + 1, :])
               for comp in range(2)]
        o = out[0] - lam * out[1]
        ms = jnp.mean(o * o, axis=0, keepdims=True)
        y = o * lax.rsqrt(ms + NORM_EPS) * g_ref[...] * (1.0 - lam_init)
        o_ref[0, rows(tile), :] = y.T.astype(o_ref.dtype)


def _alibi_operands():
    heads = np.arange(ATT_HEADS, dtype=np.float32)
    slope = (2.0 ** (-8.0 * (heads + 1.0) / ATT_HEADS)).astype(np.float32) * np.float32(LOG2E)
    pieces, rest = [], slope
    for _ in range(3):
        piece = rest.astype(BF16).astype(np.float32)
        pieces.append(piece)
        rest = rest - piece
    cols = [np.float32(POS_RADIX) * p for p in pieces] + pieces
    coef = np.zeros((ATT_HEADS, 1, LANES), np.float32)
    coef[:, 0, :len(cols)] = np.stack(cols, axis=1)
    row = np.arange(ATTN_TILE)
    pos = np.zeros((ATTN_TILE, LANES), np.float32)
    pos[:, 0:3] = (row // POS_RADIX)[:, None]
    pos[:, 3:6] = (row % POS_RADIX)[:, None]
    slope = np.broadcast_to(slope[:, None, None], (ATT_HEADS, 1, LANES))
    return jnp.asarray(slope), jnp.asarray(coef), jnp.asarray(pos, dtype=BF16)


def _mixers(q, k, vt, k_meta, vt_meta, lam_vecs, subln_g, xbct, dtt, zt, xbct_meta, dtt_meta,
            ssd_params, *, lam_init):
    nb, seq, _ = q.shape
    t = ATTN_TILE
    hc = ATT_HEAD_COLS
    n_tiles = seq // t
    steps = n_tiles // 2
    assert n_tiles % 2 == 0 and ATT_HEADS * steps == seq // BLOCK
    slopes, coef, pos = _alibi_operands()
    per_head_rows = pl.BlockSpec((1, seq, hc), lambda b, h, j: (b, 0, h))
    chunk = lambda rows: pl.BlockSpec((1, rows, BLOCK), lambda b, h, j: (b, 0, h * steps + j))
    return pl.pallas_call(
        functools.partial(_mixer_kernel, lam_init=lam_init, n_tiles=n_tiles),
        grid=(nb, ATT_HEADS, steps),
        in_specs=[per_head_rows, per_head_rows,
                  pl.BlockSpec((1, hc, seq), lambda b, h, j: (b, h, 0)),
                  pl.BlockSpec((1, BLOCK, hc), lambda b, h, j: (0, 0, h)),
                  pl.BlockSpec((1, hc, BLOCK), lambda b, h, j: (0, h, 0)),
                  pl.BlockSpec((1, 1, LANES), lambda b, h, j: (h, 0, 0)),
                  pl.BlockSpec((1, 1, LANES), lambda b, h, j: (h, 0, 0)),
                  _resident(pos.shape), _resident(lam_vecs.shape), _resident(subln_g.shape),
                  chunk(SSD_CONV_DIM), chunk(SSD_HEADS), chunk(SSD_INNER),
                  _resident(xbct_meta.shape), _resident(dtt_meta.shape)]
                 + [_resident(a.shape) for a in ssd_params],
        out_specs=[per_head_rows,
                   pl.BlockSpec((1, BLOCK, SSD_INNER), lambda b, h, j: (b, h * steps + j, 0))],
        out_shape=[jax.ShapeDtypeStruct((nb, seq, ATT_V), BF16),
                   jax.ShapeDtypeStruct((nb, seq, SSD_INNER), BF16)],
        scratch_shapes=[pltpu.VMEM((2, 2, t, 2 * LANES), BF16), pltpu.VMEM((2, 2, 1, t), F32),
                        pltpu.VMEM((2, 2, hc + DENOM_ROWS, t), F32)]
                       + [pltpu.VMEM((2, t, t), F32), pltpu.VMEM((2, 1, t), F32)] * 2
                       + [pltpu.VMEM((SSD_INNER, SSD_STATE), F32),
                          pltpu.VMEM((SSD_INNER, BLOCK), F32)],
        compiler_params=pltpu.CompilerParams(
            dimension_semantics=("parallel", "arbitrary", "arbitrary"),
            vmem_limit_bytes=VMEM_LIMIT_BYTES),
        name="mixers",
    )(q, k, vt, k_meta, vt_meta, slopes, coef, pos, lam_vecs, subln_g, xbct, dtt, zt,
      xbct_meta, dtt_meta, *ssd_params)


def _merge_ffn_kernel(x_ref, ys_ref, ya_ref, gates_ref, gbias_ref, ws_ref, wa_ref, wo_ref,
                      nffn_ref, wg_ref, wu_ref, wd_ref, nfin_ref, o_ref):
    d = x_ref.shape[-1]
    gates = _sigmoid(gates_ref[...].astype(F32) + gbias_ref[...])
    merged = (gates[:, :d] * _dot(ys_ref[...], ws_ref[...])
              + gates[:, d:] * _dot(ya_ref[...], wa_ref[...]))
    h = x_ref[...] + _dot(merged.astype(BF16), wo_ref[...])
    u = h * lax.rsqrt(jnp.mean(h * h, axis=-1, keepdims=True) + NORM_EPS) * nffn_ref[...]
    ub = u.astype(BF16)
    hidden = _silu(_dot(ub, wg_ref[...])) * _dot(ub, wu_ref[...])
    h = h + _dot(hidden.astype(BF16), wd_ref[...])
    o_ref[...] = h * lax.rsqrt(jnp.mean(h * h, axis=-1, keepdims=True) + NORM_EPS) * nfin_ref[...]


def _merge_ffn(x2, ys, ya, gates, consts):
    n, d = x2.shape
    rows = FFN_ROWS
    tile = lambda cols: pl.BlockSpec((rows, cols), lambda i: (i, 0))
    gbias, ws, wa, wo, nffn, wg, wu, wd, nfin = consts
    return pl.pallas_call(
        _merge_ffn_kernel,
        grid=(n // rows,),
        in_specs=[tile(d), tile(ys.shape[1]), tile(ya.shape[1]), tile(gates.shape[1])]
                 + [_resident(a.shape) for a in consts],
        out_specs=tile(d),
        out_shape=jax.ShapeDtypeStruct((n, d), F32),
        compiler_params=pltpu.CompilerParams(
            dimension_semantics=("parallel",), vmem_limit_bytes=VMEM_LIMIT_BYTES),
        name="merge_ffn",
    )(x2, ys, ya, gates, *consts)


def _lane_bcast(v, n=LANES):
    return jnp.broadcast_to(v.astype(F32)[:, None], (v.shape[0], n))


def _layer(x, meta_chunk, l, norm_mix_g, w_in, gate_bias, conv_w, conv_b, dt_bias, a_log, d_skip,
           ssd_norm_g, lam_vecs, subln_g, w_ssd_branch, w_attn_branch, w_out, norm_ffn_g,
           w_gate_ffn, w_up_ffn, w_down_ffn, norm_final_g):
    nb, seq, d = x.shape
    sizes = [SSD_INNER, SSD_CONV_DIM, SSD_HEADS, ATT_QK, ATT_QK, ATT_V, 2 * d]
    offs = [0]
    for s in sizes:
        offs.append(offs[-1] + s)
    col = lambda idx: w_in[:, offs[idx]:offs[idx + 1]].astype(BF16)
    w = {"z_t": col(0).T, "xbc_t": col(1).T, "dt_t": col(2).T, "q": col(3), "k": col(4),
         "v_t": col(5).T, "gates": col(6)}
    g_mix = norm_mix_g.astype(F32)[None, :]

    conv_taps = jnp.stack([_lane_bcast(conv_w[:, t]) for t in range(SSD_CONV)])
    conv_bias = _lane_bcast(conv_b)
    _, k_meta, _, xbct_meta, _, vt_meta, dtt_meta, meta_tail = _in_projection(
        meta_chunk, g_mix, w, conv_taps, conv_bias, jnp.zeros((1, SSD_CONV_DIM, BLOCK), F32),
        rows=BLOCK, masked_rows=PAD_FRONT)
    q, k, gates, xbct, zt, vt, dtt, _ = _in_projection(
        x, g_mix, w, conv_taps, conv_bias, meta_tail, rows=INPROJ_ROWS, masked_rows=0)

    ssd_params = [_lane_bcast(dt_bias), _lane_bcast(a_log),
                  _lane_bcast(jnp.repeat(d_skip, SSD_HEADDIM)), _lane_bcast(ssd_norm_g)]
    lam_init = 0.8 - 0.6 * math.exp(-0.3 * l)
    y_att, y_ssd = _mixers(q, k, vt, k_meta, vt_meta, lam_vecs.astype(F32),
                           _lane_bcast(subln_g, ATTN_TILE), xbct, dtt, zt, xbct_meta, dtt_meta,
                           ssd_params, lam_init=lam_init)

    consts = [gate_bias.astype(F32)[None, :], w_ssd_branch.astype(BF16),
              w_attn_branch.astype(BF16), w_out.astype(BF16), norm_ffn_g.astype(F32)[None, :],
              w_gate_ffn.astype(BF16), w_up_ffn.astype(BF16), w_down_ffn.astype(BF16),
              norm_final_g.astype(F32)[None, :]]
    out = _merge_ffn(x.reshape(nb * seq, d), y_ssd.reshape(nb * seq, SSD_INNER),
                     y_att.reshape(nb * seq, ATT_V), gates.reshape(nb * seq, 2 * d), consts)
    return out.reshape(nb, seq, d)


def kernel(x, meta_tokens, norm_mix_g, w_in, gate_bias, conv_w, conv_b, dt_bias, a_log, d_skip, ssd_norm_g, lambda_q1, lambda_k1, lambda_q2, lambda_k2, subln_g, w_ssd_branch, w_attn_branch, w_out, norm_ffn_g, w_gate_ffn, w_up_ffn, w_down_ffn, norm_final_g):
    depth = w_in.shape[0]
    assert depth == 1, "the fused final norm assumes a single layer"
    d = x.shape[-1]
    meta_chunk = jnp.concatenate(
        [jnp.zeros((PAD_FRONT, d), x.dtype), meta_tokens.astype(x.dtype)], axis=0)[None]
    l = 0
    lam_vecs = jnp.stack([lambda_q1[l], lambda_k1[l], lambda_q2[l], lambda_k2[l]])
    return _layer(x, meta_chunk, l, norm_mix_g[l], w_in[l], gate_bias[l], conv_w[l], conv_b[l],
                  dt_bias[l], a_log[l], d_skip[l], ssd_norm_g[l], lam_vecs, subln_g[l],
                  w_ssd_branch[l], w_attn_branch[l], w_out[l], norm_ffn_g[l], w_gate_ffn[l],
                  w_up_ffn[l], w_down_ffn[l], norm_final_g)
```

```python
import functools
import math

import jax
import jax.numpy as jnp
import numpy as np
from jax import lax
from jax.experimental import pallas as pl
from jax.experimental.pallas import tpu as pltpu

F32 = jnp.float32
BF16 = jnp.bfloat16

N_META = 16
BLOCK = 128
PAD_FRONT = BLOCK - N_META
NORM_EPS = 1e-6
NEG = -1e30

SSD_HEADDIM = 64
SSD_HEADS = 16
SSD_INNER = SSD_HEADS * SSD_HEADDIM
SSD_GROUPS = 2
SSD_STATE = 128
SSD_CONV = 4
SSD_BC = SSD_GROUPS * SSD_STATE
SSD_CONV_DIM = SSD_INNER + 2 * SSD_BC
HEADS_PER_GROUP = SSD_HEADS // SSD_GROUPS
GROUP_INNER = HEADS_PER_GROUP * SSD_HEADDIM

ATT_HEADS = 8
ATT_HEAD_DIM = 64
ATT_QK = ATT_HEADS * 2 * ATT_HEAD_DIM
ATT_V = ATT_HEADS * 2 * ATT_HEAD_DIM
ATT_HEAD_COLS = 2 * ATT_HEAD_DIM

LOG2E = math.log2(math.e)
Q_SCALE = ATT_HEAD_DIM ** -0.5 * LOG2E

LANES = 128
VMEM_LIMIT_BYTES = 56 * 1024 * 1024
INPROJ_ROWS = 512
CONV_ROWS = 64
ATTN_TILE = 512
ATTN_RUN = 4
POS_RADIX = 16
DENOM_ROWS = 16
FFN_ROWS = 512


def _resident(shape):
    zeros = (0,) * len(shape)
    return pl.BlockSpec(shape, lambda *_: zeros, pipeline_mode=pl.Buffered(1))


def _sigmoid(x):
    return 1.0 / (1.0 + jnp.exp2(x * (-LOG2E)))


def _silu(x):
    return x * _sigmoid(x)


def _dot(a, b):
    return jnp.dot(a, b, preferred_element_type=F32)


def _dot_nt(a, b):
    return lax.dot_general(a, b, (((1,), (1,)), ((), ())), preferred_element_type=F32)


def _dot_tn(a, b):
    return lax.dot_general(a, b, (((0,), (0,)), ((), ())), preferred_element_type=F32)


def _inproj_kernel(x_ref, g_ref, wq_ref, wk_ref, wg_ref, wxbct_ref, wzt_ref, wvt_ref, wdtt_ref,
                   convw_ref, convb_ref, hist_ref, q_ref, k_ref, gates_ref, xbct_ref, zt_ref,
                   vt_ref, dtt_ref, tail_ref, raw_ref, *, masked_rows):
    rows = x_ref.shape[1]
    x = x_ref[0]
    ms = jnp.mean(x * x, axis=-1, keepdims=True)
    u = x * lax.rsqrt(ms + NORM_EPS) * g_ref[...]
    if masked_rows:
        row = lax.broadcasted_iota(jnp.int32, u.shape, 0)
        u = jnp.where(row >= masked_rows, u, 0.0)
    ub = u.astype(BF16)

    @pl.when(pl.program_id(1) == 0)
    def _():
        raw_ref[:, 0:BLOCK] = hist_ref[0]

    raw_ref[:, BLOCK:] = _dot_nt(wxbct_ref[...], ub)
    q_ref[0] = (_dot(ub, wq_ref[...]) * Q_SCALE).astype(q_ref.dtype)
    k_ref[0] = _dot(ub, wk_ref[...]).astype(k_ref.dtype)
    gates_ref[0] = _dot(ub, wg_ref[...]).astype(gates_ref.dtype)
    zt_ref[0] = _dot_nt(wzt_ref[...], ub).astype(zt_ref.dtype)
    vt_ref[0] = _dot_nt(wvt_ref[...], ub).astype(vt_ref.dtype)
    dtt_ref[0] = _dot_nt(wdtt_ref[...], ub)
    lane = lax.broadcasted_iota(jnp.int32, (CONV_ROWS, BLOCK), 1)
    for blk in range(SSD_CONV_DIM // CONV_ROWS):
        rs = slice(blk * CONV_ROWS, (blk + 1) * CONV_ROWS)
        taps = [convw_ref[t, rs, :] for t in range(SSD_CONV)]
        bias = convb_ref[rs, :]
        for cb in range(rows // BLOCK):
            x_prev = raw_ref[rs, cb * BLOCK:(cb + 1) * BLOCK]
            xc = raw_ref[rs, (cb + 1) * BLOCK:(cb + 2) * BLOCK]
            acc = bias + taps[SSD_CONV - 1] * xc
            for j in range(1, SSD_CONV):
                joined = jnp.where(lane >= BLOCK - j, x_prev, xc)
                acc = acc + taps[SSD_CONV - 1 - j] * pltpu.roll(joined, j, 1)
            xbct_ref[0, rs, cb * BLOCK:(cb + 1) * BLOCK] = _silu(acc)
        tail = raw_ref[rs, rows:rows + BLOCK]
        raw_ref[rs, 0:BLOCK] = tail
        tail_ref[0, rs, :] = tail


def _in_projection(h, g, w, conv_w, conv_b, hist, *, rows, masked_rows):
    nb, seq, d = h.shape
    grid = (nb, seq // rows)
    row_major = lambda cols, dt: (jax.ShapeDtypeStruct((nb, seq, cols), dt),
                                  pl.BlockSpec((1, rows, cols), lambda b, i: (b, i, 0)))
    chan_major = lambda cols, dt: (jax.ShapeDtypeStruct((nb, cols, seq), dt),
                                   pl.BlockSpec((1, cols, rows), lambda b, i: (b, 0, i)))
    tail = (jax.ShapeDtypeStruct((nb, SSD_CONV_DIM, BLOCK), F32),
            pl.BlockSpec((1, SSD_CONV_DIM, BLOCK), lambda b, i: (b, 0, 0)))
    outs = [row_major(ATT_QK, BF16), row_major(ATT_QK, BF16), row_major(2 * d, BF16),
            chan_major(SSD_CONV_DIM, F32), chan_major(SSD_INNER, BF16), chan_major(ATT_V, BF16),
            chan_major(SSD_HEADS, F32), tail]
    consts = [g, w["q"], w["k"], w["gates"], w["xbc_t"], w["z_t"], w["v_t"], w["dt_t"],
              conv_w, conv_b, hist]
    return pl.pallas_call(
        functools.partial(_inproj_kernel, masked_rows=masked_rows),
        grid=grid,
        in_specs=[pl.BlockSpec((1, rows, d), lambda b, i: (b, i, 0))]
                 + [_resident(a.shape) for a in consts],
        out_specs=[o[1] for o in outs],
        out_shape=[o[0] for o in outs],
        scratch_shapes=[pltpu.VMEM((SSD_CONV_DIM, BLOCK + rows), F32)],
        compiler_params=pltpu.CompilerParams(
            dimension_semantics=("parallel", "arbitrary"), vmem_limit_bytes=VMEM_LIMIT_BYTES),
        name="in_projection",
    )(h, *consts)


def _ssd_scan(xbc_ref, dt_raw, zt_ref, y_ref, dtb_ref, alog_ref, dskip_ref, ng_ref, state_ref,
              ygate_ref, *, first):
    v = dt_raw + dtb_ref[...]
    dt = jnp.maximum(v, 0.0) + jnp.log1p(jnp.exp(-jnp.abs(v)))
    if first:
        step = lax.broadcasted_iota(jnp.int32, dt.shape, 1)
        dt = jnp.where(step >= PAD_FRONT, dt, 0.0)
    da = dt * (-jnp.exp(alog_ref[...]))
    r = lax.broadcasted_iota(jnp.int32, (BLOCK, BLOCK), 0)
    c = lax.broadcasted_iota(jnp.int32, (BLOCK, BLOCK), 1)
    upper = r <= c
    acs = jnp.dot(da, upper.astype(F32), preferred_element_type=F32,
                  precision=lax.Precision.HIGHEST)
    last = acs[:, BLOCK - 1:BLOCK]
    to_end = jnp.exp(last - acs) * dt
    chunk_decay = jnp.broadcast_to(jnp.exp(last), acs.shape)
    if not first:
        acs_col = jnp.concatenate([acs, jnp.zeros((BLOCK - SSD_HEADS, BLOCK), F32)], axis=0).T
        grow = jnp.exp(acs)
        ssq = jnp.zeros((1, BLOCK), F32)

    for g in range(SSD_GROUPS):
        b_rows = slice(SSD_INNER + g * SSD_STATE, SSD_INNER + (g + 1) * SSD_STATE)
        c_rows = slice(SSD_INNER + SSD_BC + g * SSD_STATE, SSD_INNER + SSD_BC + (g + 1) * SSD_STATE)
        bg = xbc_ref[0, b_rows, :].astype(BF16)
        heads = range(g * HEADS_PER_GROUP, (g + 1) * HEADS_PER_GROUP)
        grp = slice(g * GROUP_INNER, (g + 1) * GROUP_INNER)
        per_row = lambda a: jnp.concatenate(
            [jnp.broadcast_to(a[h:h + 1, :], (SSD_HEADDIM, BLOCK)) for h in heads], axis=0)
        inject = _dot_nt((xbc_ref[0, grp, :] * per_row(to_end)).astype(BF16), bg)
        if first:
            state_ref[grp, :] = inject
            continue
        cg = xbc_ref[0, c_rows, :].astype(BF16)
        cb_t = _dot_tn(bg, cg)
        st = state_ref[grp, :]
        state_ref[grp, :] = st * per_row(chunk_decay) + inject
        y_off = _dot(st.astype(BF16), cg)
        for e, h in enumerate(heads):
            rows = slice(h * SSD_HEADDIM, (h + 1) * SSD_HEADDIM)
            xs = xbc_ref[0, rows, :]
            seg = acs[h:h + 1, :] - acs_col[:, h:h + 1]
            g_t = (cb_t * jnp.where(upper, jnp.exp(seg), 0.0)).astype(BF16)
            y = _dot((xs * dt[h:h + 1, :]).astype(BF16), g_t)
            y = (y + y_off[e * SSD_HEADDIM:(e + 1) * SSD_HEADDIM] * grow[h:h + 1, :]
                 + xs * dskip_ref[rows, :])
            y = y * _silu(zt_ref[0, rows, :].astype(F32))
            ssq = ssq + jnp.sum(y * y, axis=0, keepdims=True)
            ygate_ref[rows, :] = y

    if not first:
        scale = lax.rsqrt(ssq * (1.0 / SSD_INNER) + NORM_EPS)
        for blk in range(SSD_INNER // BLOCK):
            rs = slice(blk * BLOCK, (blk + 1) * BLOCK)
            y = ygate_ref[rs, :] * scale * ng_ref[rs, :]
            y_ref[0, :, rs] = y.T.astype(y_ref.dtype)


def _mixer_kernel(q_ref, k_ref, vt_ref, k_meta_ref, vt_meta_ref, slope_ref, coef_ref, pos_ref,
                  lam_ref, g_ref, xbct_ref, dtt_ref, zt_ref, xbct_meta_ref, dtt_meta_ref,
                  dtb_ref, alog_ref, dskip_ref, ng_ref, o_ref, y_ref, qa_ref, m_ref, acc_ref,
                  sa_ref, ca_ref, sb_ref, cb_ref, state_ref, ygate_ref, *, lam_init, n_tiles):
    j = pl.program_id(2)
    scan = functools.partial(_ssd_scan, zt_ref=zt_ref, y_ref=y_ref, dtb_ref=dtb_ref,
                             alog_ref=alog_ref, dskip_ref=dskip_ref, ng_ref=ng_ref,
                             state_ref=state_ref, ygate_ref=ygate_ref)

    @pl.when((pl.program_id(1) == 0) & (j == 0))
    def _():
        scan(xbct_meta_ref, dtt_meta_ref[0], first=True)

    t = ATTN_TILE
    hc = ATT_HEAD_COLS
    q_tiles = (j, n_tiles - 1 - j)
    slope = slope_ref[0][:, 0:1]
    key_row = lax.broadcasted_iota(jnp.int32, (t, t), 0)
    qry_col = lax.broadcasted_iota(jnp.int32, (t, t), 1)
    meta_row = lax.broadcasted_iota(jnp.int32, (BLOCK, t), 0)

    def rows(tile):
        return pl.ds(pl.multiple_of(tile * t, t), t)

    coef = jnp.broadcast_to(coef_ref[0], (t, LANES)).astype(BF16)
    lane = lax.broadcasted_iota(jnp.int32, (t, hc), 1)
    for w, tile in enumerate(q_tiles):
        qf = q_ref[0, rows(tile), :]
        for comp, keep in enumerate((lane < ATT_HEAD_DIM, lane >= ATT_HEAD_DIM)):
            qa_ref[w, comp] = jnp.concatenate([jnp.where(keep, qf, jnp.zeros_like(qf)), coef],
                                              axis=1)
    m_ref[...] = jnp.full(m_ref.shape, NEG, F32)
    acc_ref[...] = jnp.zeros(acc_ref.shape, F32)

    def scores(kt, w, mask):
        k_aug = jnp.concatenate([kt, pos_ref[0:kt.shape[0], :]], axis=1)
        out = []
        for comp in range(2):
            s = _dot_nt(k_aug, qa_ref[w, comp])
            out.append(s if mask is None else jnp.where(mask, s, NEG))
        return out

    def fold(w, s_of, cmax_of, vtt, key_start):
        sigma = slope * lax.convert_element_type(key_start, F32)
        v_aug = jnp.concatenate([vtt, jnp.ones((DENOM_ROWS, vtt.shape[1]), BF16)], axis=0)
        for comp in range(2):
            m_old = m_ref[w, comp]
            m_new = jnp.maximum(m_old, cmax_of(comp) + sigma)
            alpha = jnp.exp2(m_old - m_new)
            pr = jnp.exp2(s_of(comp) - (m_new - sigma))
            acc_ref[w, comp] = alpha * acc_ref[w, comp] + _dot(v_aug, pr.astype(BF16))
            m_ref[w, comp] = m_new

    n_items = n_tiles + 1
    assert (n_items - 1) % ATTN_RUN == 0 and ATTN_RUN % 2 == 0

    def item(r):
        r = jnp.asarray(r, jnp.int32)
        is_first, is_last = r == 0, r == n_items - 1
        w_full = (r - 1 >= j).astype(jnp.int32)
        w = jnp.where(is_last, 1, jnp.where(is_first, 0, w_full))
        key_tile = jnp.where(is_first, q_tiles[0],
                             jnp.where(is_last, q_tiles[1], r - 1 - w_full * j))
        return w, key_tile, jnp.where(w == 0, q_tiles[0], q_tiles[1])

    slots = ((sa_ref, ca_ref), (sb_ref, cb_ref))

    def stage(r, parity, masked):
        w, key_tile, _ = item(r)
        stage_ref, cmax_ref = slots[parity]
        mask = (key_row <= qry_col) if masked else None
        for comp, s in enumerate(scores(k_ref[0, rows(key_tile), :], w, mask)):
            stage_ref[comp] = s
            cmax_ref[comp] = jnp.max(s, axis=0, keepdims=True)

    def fold_staged(r, parity):
        w, key_tile, q_tile = item(r)
        stage_ref, cmax_ref = slots[parity]
        fold(w, lambda comp: stage_ref[comp], lambda comp: cmax_ref[comp],
             vt_ref[0, :, rows(key_tile)], (key_tile - q_tile) * t)

    s_meta = [scores(k_meta_ref[0], w, meta_row >= PAD_FRONT) for w in range(2)]
    stage(0, 0, True)
    for w, tile in enumerate(q_tiles):
        fold(w, lambda comp: s_meta[w][comp],
             lambda comp: jnp.max(s_meta[w][comp], axis=0, keepdims=True),
             vt_meta_ref[0], -(BLOCK + tile * t))

    def run(base, ends_on_diagonal):
        for c in range(1, ATTN_RUN + 1):
            stage(base + c, c % 2, ends_on_diagonal and c == ATTN_RUN)
            fold_staged(base + c - 1, (c - 1) % 2)

    def run_body(u, carry):
        run(u * ATTN_RUN, False)
        return carry

    n_runs = (n_items - 1) // ATTN_RUN
    lax.fori_loop(0, n_runs - 1, run_body, 0)
    scan(xbct_ref, dtt_ref[0], first=False)
    run((n_runs - 1) * ATTN_RUN, True)
    fold_staged(n_items - 1, (n_items - 1) % 2)

    lam = (jnp.exp(jnp.sum(lam_ref[0:1] * lam_ref[1:2], axis=-1, keepdims=True))
           - jnp.exp(jnp.sum(lam_ref[2:3] * lam_ref[3:4], axis=-1, keepdims=True)) + lam_init)
    for w, tile in enumerate(q_tiles):
        out = [acc_ref[w, comp, 0:hc, :] * (1.0 / acc_ref[w, comp, hc:hc + 1, :])
               for comp in range(2)]
        o = out[0] - lam * out[1]
        ms = jnp.mean(o * o, axis=0, keepdims=True)
        y = o * lax.rsqrt(ms + NORM_EPS) * g_ref[...] * (1.0 - lam_init)
        o_ref[0, rows(tile), :] = y.T.astype(o_ref.dtype)


def _alibi_operands():
    heads = np.arange(ATT_HEADS, dtype=np.float32)
    slope = (2.0 ** (-8.0 * (heads + 1.0) / ATT_HEADS)).astype(np.float32) * np.float32(LOG2E)
    pieces, rest = [], slope
    for _ in range(3):
        piece = rest.astype(BF16).astype(np.float32)
        pieces.append(piece)
        rest = rest - piece
    cols = [np.float32(POS_RADIX) * p for p in pieces] + pieces
    coef = np.zeros((ATT_HEADS, 1, LANES), np.float32)
    coef[:, 0, :len(cols)] = np.stack(cols, axis=1)
    row = np.arange(ATTN_TILE)
    pos = np.zeros((ATTN_TILE, LANES), np.float32)
    pos[:, 0:3] = (row // POS_RADIX)[:, None]
    pos[:, 3:6] = (row % POS_RADIX)[:, None]
    slope = np.broadcast_to(slope[:, None, None], (ATT_HEADS, 1, LANES))
    return jnp.asarray(slope), jnp.asarray(coef), jnp.asarray(pos, dtype=BF16)


def _mixers(q, k, vt, k_meta, vt_meta, lam_vecs, subln_g, xbct, dtt, zt, xbct_meta, dtt_meta,
            ssd_params, *, lam_init):
    nb, seq, _ = q.shape
    t = ATTN_TILE
    hc = ATT_HEAD_COLS
    n_tiles = seq // t
    steps = n_tiles // 2
    assert n_tiles % 2 == 0 and ATT_HEADS * steps == seq // BLOCK
    slopes, coef, pos = _alibi_operands()
    per_head_rows = pl.BlockSpec((1, seq, hc), lambda b, h, j: (b, 0, h))
    chunk = lambda rows: pl.BlockSpec((1, rows, BLOCK), lambda b, h, j: (b, 0, h * steps + j))
    return pl.pallas_call(
        functools.partial(_mixer_kernel, lam_init=lam_init, n_tiles=n_tiles),
        grid=(nb, ATT_HEADS, steps),
        in_specs=[per_head_rows, per_head_rows,
                  pl.BlockSpec((1, hc, seq), lambda b, h, j: (b, h, 0)),
                  pl.BlockSpec((1, BLOCK, hc), lambda b, h, j: (0, 0, h)),
                  pl.BlockSpec((1, hc, BLOCK), lambda b, h, j: (0, h, 0)),
                  pl.BlockSpec((1, 1, LANES), lambda b, h, j: (h, 0, 0)),
                  pl.BlockSpec((1, 1, LANES), lambda b, h, j: (h, 0, 0)),
                  _resident(pos.shape), _resident(lam_vecs.shape), _resident(subln_g.shape),
                  chunk(SSD_CONV_DIM), chunk(SSD_HEADS), chunk(SSD_INNER),
                  _resident(xbct_meta.shape), _resident(dtt_meta.shape)]
                 + [_resident(a.shape) for a in ssd_params],
        out_specs=[per_head_rows,
                   pl.BlockSpec((1, BLOCK, SSD_INNER), lambda b, h, j: (b, h * steps + j, 0))],
        out_shape=[jax.ShapeDtypeStruct((nb, seq, ATT_V), BF16),
                   jax.ShapeDtypeStruct((nb, seq, SSD_INNER), BF16)],
        scratch_shapes=[pltpu.VMEM((2, 2, t, 2 * LANES), BF16), pltpu.VMEM((2, 2, 1, t), F32),
                        pltpu.VMEM((2, 2, hc + DENOM_ROWS, t), F32)]
                       + [pltpu.VMEM((2, t, t), F32), pltpu.VMEM((2, 1, t), F32)] * 2
                       + [pltpu.VMEM((SSD_INNER, SSD_STATE), F32),
                          pltpu.VMEM((SSD_INNER, BLOCK), F32)],
        compiler_params=pltpu.CompilerParams(
            dimension_semantics=("parallel", "arbitrary", "arbitrary"),
            vmem_limit_bytes=VMEM_LIMIT_BYTES),
        name="mixers",
    )(q, k, vt, k_meta, vt_meta, slopes, coef, pos, lam_vecs, subln_g, xbct, dtt, zt,
      xbct_meta, dtt_meta, *ssd_params)


def _merge_ffn_kernel(x_ref, ys_ref, ya_ref, gates_ref, gbias_ref, ws_ref, wa_ref, wo_ref,
                      nffn_ref, wg_ref, wu_ref, wd_ref, nfin_ref, o_ref):
    d = x_ref.shape[-1]
    gates = _sigmoid(gates_ref[...].astype(F32) + gbias_ref[...])
    merged = (gates[:, :d] * _dot(ys_ref[...], ws_ref[...])
              + gates[:, d:] * _dot(ya_ref[...], wa_ref[...]))
    h = x_ref[...] + _dot(merged.astype(BF16), wo_ref[...])
    u = h * lax.rsqrt(jnp.mean(h * h, axis=-1, keepdims=True) + NORM_EPS) * nffn_ref[...]
    ub = u.astype(BF16)
    hidden = _silu(_dot(ub, wg_ref[...])) * _dot(ub, wu_ref[...])
    h = h + _dot(hidden.astype(BF16), wd_ref[...])
    o_ref[...] = h * lax.rsqrt(jnp.mean(h * h, axis=-1, keepdims=True) + NORM_EPS) * nfin_ref[...]


def _merge_ffn(x2, ys, ya, gates, consts):
    n, d = x2.shape
    rows = FFN_ROWS
    tile = lambda cols: pl.BlockSpec((rows, cols), lambda i: (i, 0))
    gbias, ws, wa, wo, nffn, wg, wu, wd, nfin = consts
    return pl.pallas_call(
        _merge_ffn_kernel,
        grid=(n // rows,),
        in_specs=[tile(d), tile(ys.shape[1]), tile(ya.shape[1]), tile(gates.shape[1])]
                 + [_resident(a.shape) for a in consts],
        out_specs=tile(d),
        out_shape=jax.ShapeDtypeStruct((n, d), F32),
        compiler_params=pltpu.CompilerParams(
            dimension_semantics=("parallel",), vmem_limit_bytes=VMEM_LIMIT_BYTES),
        name="merge_ffn",
    )(x2, ys, ya, gates, *consts)


def _lane_bcast(v, n=LANES):
    return jnp.broadcast_to(v.astype(F32)[:, None], (v.shape[0], n))


def _layer(x, meta_chunk, l, norm_mix_g, w_in, gate_bias, conv_w, conv_b, dt_bias, a_log, d_skip,
           ssd_norm_g, lam_vecs, subln_g, w_ssd_branch, w_attn_branch, w_out, norm_ffn_g,
           w_gate_ffn, w_up_ffn, w_down_ffn, norm_final_g):
    nb, seq, d = x.shape
    sizes = [SSD_INNER, SSD_CONV_DIM, SSD_HEADS, ATT_QK, ATT_QK, ATT_V, 2 * d]
    offs = [0]
    for s in sizes:
        offs.append(offs[-1] + s)
    col = lambda idx: w_in[:, offs[idx]:offs[idx + 1]].astype(BF16)
    w = {"z_t": col(0).T, "xbc_t": col(1).T, "dt_t": col(2).T, "q": col(3), "k": col(4),
         "v_t": col(5).T, "gates": col(6)}
    g_mix = norm_mix_g.astype(F32)[None, :]

    conv_taps = jnp.stack([_lane_bcast(conv_w[:, t]) for t in range(SSD_CONV)])
    conv_bias = _lane_bcast(conv_b)
    _, k_meta, _, xbct_meta, _, vt_meta, dtt_meta, meta_tail = _in_projection(
        meta_chunk, g_mix, w, conv_taps, conv_bias, jnp.zeros((1, SSD_CONV_DIM, BLOCK), F32),
        rows=BLOCK, masked_rows=PAD_FRONT)
    q, k, gates, xbct, zt, vt, dtt, _ = _in_projection(
        x, g_mix, w, conv_taps, conv_bias, meta_tail, rows=INPROJ_ROWS, masked_rows=0)

    ssd_params = [_lane_bcast(dt_bias), _lane_bcast(a_log),
                  _lane_bcast(jnp.repeat(d_skip, SSD_HEADDIM)), _lane_bcast(ssd_norm_g)]
    lam_init = 0.8 - 0.6 * math.exp(-0.3 * l)
    y_att, y_ssd = _mixers(q, k, vt, k_meta, vt_meta, lam_vecs.astype(F32),
                           _lane_bcast(subln_g, ATTN_TILE), xbct, dtt, zt, xbct_meta, dtt_meta,
                           ssd_params, lam_init=lam_init)

    consts = [gate_bias.astype(F32)[None, :], w_ssd_branch.astype(BF16),
              w_attn_branch.astype(BF16), w_out.astype(BF16), norm_ffn_g.astype(F32)[None, :],
              w_gate_ffn.astype(BF16), w_up_ffn.astype(BF16), w_down_ffn.astype(BF16),
              norm_final_g.astype(F32)[None, :]]
    out = _merge_ffn(x.reshape(nb * seq, d), y_ssd.reshape(nb * seq, SSD_INNER),
                     y_att.reshape(nb * seq, ATT_V), gates.reshape(nb * seq, 2 * d), consts)
    return out.reshape(nb, seq, d)


def kernel(x, meta_tokens, norm_mix_g, w_in, gate_bias, conv_w, conv_b, dt_bias, a_log, d_skip, ssd_norm_g, lambda_q1, lambda_k1, lambda_q2, lambda_k2, subln_g, w_ssd_branch, w_attn_branch, w_out, norm_ffn_g, w_gate_ffn, w_up_ffn, w_down_ffn, norm_final_g):
    depth = w_in.shape[0]
    assert depth == 1, "the fused final norm assumes a single layer"
    d = x.shape[-1]
    meta_chunk = jnp.concatenate(
        [jnp.zeros((PAD_FRONT, d), x.dtype), meta_tokens.astype(x.dtype)], axis=0)[None]
    l = 0
    lam_vecs = jnp.stack([lambda_q1[l], lambda_k1[l], lambda_q2[l], lambda_k2[l]])
    return _layer(x, meta_chunk, l, norm_mix_g[l], w_in[l], gate_bias[l], conv_w[l], conv_b[l],
                  dt_bias[l], a_log[l], d_skip[l], ssd_norm_g[l], lam_vecs, subln_g[l],
                  w_ssd_branch[l], w_attn_branch[l], w_out[l], norm_ffn_g[l], w_gate_ffn[l],
                  w_up_ffn[l], w_down_ffn[l], norm_final_g)
```

```python
import functools
import math

import jax
import jax.numpy as jnp
import numpy as np
from jax import lax
from jax.experimental import pallas as pl
from jax.experimental.pallas import tpu as pltpu

F32 = jnp.float32
BF16 = jnp.bfloat16

N_META = 16
BLOCK = 128
PAD_FRONT = BLOCK - N_META
NORM_EPS = 1e-6
NEG = -1e30

SSD_HEADDIM = 64
SSD_HEADS = 16
SSD_INNER = SSD_HEADS * SSD_HEADDIM
SSD_GROUPS = 2
SSD_STATE = 128
SSD_CONV = 4
SSD_BC = SSD_GROUPS * SSD_STATE
SSD_CONV_DIM = SSD_INNER + 2 * SSD_BC
HEADS_PER_GROUP = SSD_HEADS // SSD_GROUPS
GROUP_INNER = HEADS_PER_GROUP * SSD_HEADDIM

ATT_HEADS = 8
ATT_HEAD_DIM = 64
ATT_QK = ATT_HEADS * 2 * ATT_HEAD_DIM
ATT_V = ATT_HEADS * 2 * ATT_HEAD_DIM
ATT_HEAD_COLS = 2 * ATT_HEAD_DIM

LOG2E = math.log2(math.e)
Q_SCALE = ATT_HEAD_DIM ** -0.5 * LOG2E

LANES = 128
VMEM_LIMIT_BYTES = 56 * 1024 * 1024
INPROJ_ROWS = 512
CONV_ROWS = 64
ATTN_TILE = 512
ATTN_RUN = 4
ATTN_GROUP = 4
POS_RADIX = 16
DENOM_ROWS = 16
FFN_ROWS = 512


def _resident(shape):
    zeros = (0,) * len(shape)
    return pl.BlockSpec(shape, lambda *_: zeros, pipeline_mode=pl.Buffered(1))


def _sigmoid(x):
    return 1.0 / (1.0 + jnp.exp2(x * (-LOG2E)))


def _silu(x):
    return x * _sigmoid(x)


def _dot(a, b):
    return jnp.dot(a, b, preferred_element_type=F32)


def _dot_nt(a, b):
    return lax.dot_general(a, b, (((1,), (1,)), ((), ())), preferred_element_type=F32)


def _dot_tn(a, b):
    return lax.dot_general(a, b, (((0,), (0,)), ((), ())), preferred_element_type=F32)


def _inproj_kernel(x_ref, g_ref, wq_ref, wk_ref, wg_ref, wxbct_ref, wzt_ref, wvt_ref, wdtt_ref,
                   convw_ref, convb_ref, hist_ref, q_ref, k_ref, gates_ref, xbct_ref, zt_ref,
                   vt_ref, dtt_ref, tail_ref, raw_ref, *, masked_rows):
    rows = x_ref.shape[1]
    x = x_ref[0]
    ms = jnp.mean(x * x, axis=-1, keepdims=True)
    u = x * lax.rsqrt(ms + NORM_EPS) * g_ref[...]
    if masked_rows:
        row = lax.broadcasted_iota(jnp.int32, u.shape, 0)
        u = jnp.where(row >= masked_rows, u, 0.0)
    ub = u.astype(BF16)

    @pl.when(pl.program_id(1) == 0)
    def _():
        raw_ref[:, 0:BLOCK] = hist_ref[0]

    raw_ref[:, BLOCK:] = _dot_nt(wxbct_ref[...], ub)
    q_ref[0] = (_dot(ub, wq_ref[...]) * Q_SCALE).astype(q_ref.dtype)
    k_ref[0] = _dot(ub, wk_ref[...]).astype(k_ref.dtype)
    gates_ref[0] = _dot(ub, wg_ref[...]).astype(gates_ref.dtype)
    zt_ref[0] = _dot_nt(wzt_ref[...], ub).astype(zt_ref.dtype)
    vt_ref[0] = _dot_nt(wvt_ref[...], ub).astype(vt_ref.dtype)
    dtt_ref[0] = _dot_nt(wdtt_ref[...], ub)
    lane = lax.broadcasted_iota(jnp.int32, (CONV_ROWS, BLOCK), 1)
    for blk in range(SSD_CONV_DIM // CONV_ROWS):
        rs = slice(blk * CONV_ROWS, (blk + 1) * CONV_ROWS)
        taps = [convw_ref[t, rs, :] for t in range(SSD_CONV)]
        bias = convb_ref[rs, :]
        for cb in range(rows // BLOCK):
            x_prev = raw_ref[rs, cb * BLOCK:(cb + 1) * BLOCK]
            xc = raw_ref[rs, (cb + 1) * BLOCK:(cb + 2) * BLOCK]
            acc = bias + taps[SSD_CONV - 1] * xc
            for j in range(1, SSD_CONV):
                joined = jnp.where(lane >= BLOCK - j, x_prev, xc)
                acc = acc + taps[SSD_CONV - 1 - j] * pltpu.roll(joined, j, 1)
            xbct_ref[0, rs, cb * BLOCK:(cb + 1) * BLOCK] = _silu(acc)
        tail = raw_ref[rs, rows:rows + BLOCK]
        raw_ref[rs, 0:BLOCK] = tail
        tail_ref[0, rs, :] = tail


def _in_projection(h, g, w, conv_w, conv_b, hist, *, rows, masked_rows):
    nb, seq, d = h.shape
    grid = (nb, seq // rows)
    row_major = lambda cols, dt: (jax.ShapeDtypeStruct((nb, seq, cols), dt),
                                  pl.BlockSpec((1, rows, cols), lambda b, i: (b, i, 0)))
    chan_major = lambda cols, dt: (jax.ShapeDtypeStruct((nb, cols, seq), dt),
                                   pl.BlockSpec((1, cols, rows), lambda b, i: (b, 0, i)))
    tail = (jax.ShapeDtypeStruct((nb, SSD_CONV_DIM, BLOCK), F32),
            pl.BlockSpec((1, SSD_CONV_DIM, BLOCK), lambda b, i: (b, 0, 0)))
    outs = [row_major(ATT_QK, BF16), row_major(ATT_QK, BF16), row_major(2 * d, BF16),
            chan_major(SSD_CONV_DIM, F32), chan_major(SSD_INNER, BF16), chan_major(ATT_V, BF16),
            chan_major(SSD_HEADS, F32), tail]
    consts = [g, w["q"], w["k"], w["gates"], w["xbc_t"], w["z_t"], w["v_t"], w["dt_t"],
              conv_w, conv_b, hist]
    return pl.pallas_call(
        functools.partial(_inproj_kernel, masked_rows=masked_rows),
        grid=grid,
        in_specs=[pl.BlockSpec((1, rows, d), lambda b, i: (b, i, 0))]
                 + [_resident(a.shape) for a in consts],
        out_specs=[o[1] for o in outs],
        out_shape=[o[0] for o in outs],
        scratch_shapes=[pltpu.VMEM((SSD_CONV_DIM, BLOCK + rows), F32)],
        compiler_params=pltpu.CompilerParams(
            dimension_semantics=("parallel", "arbitrary"), vmem_limit_bytes=VMEM_LIMIT_BYTES),
        name="in_projection",
    )(h, *consts)


def _ssd_scan(xbc_ref, dtt_ref, ts, zt_ref, y_ref, dtb_ref, alog_ref, dskip_ref, ng_ref,
              state_ref, ygate_ref, *, first):
    v = dtt_ref[0, :, ts] + dtb_ref[...]
    dt = jnp.maximum(v, 0.0) + jnp.log1p(jnp.exp(-jnp.abs(v)))
    if first:
        step = lax.broadcasted_iota(jnp.int32, dt.shape, 1)
        dt = jnp.where(step >= PAD_FRONT, dt, 0.0)
    da = dt * (-jnp.exp(alog_ref[...]))
    r = lax.broadcasted_iota(jnp.int32, (BLOCK, BLOCK), 0)
    c = lax.broadcasted_iota(jnp.int32, (BLOCK, BLOCK), 1)
    upper = r <= c
    acs = jnp.dot(da, upper.astype(F32), preferred_element_type=F32,
                  precision=lax.Precision.HIGHEST)
    last = acs[:, BLOCK - 1:BLOCK]
    to_end = jnp.exp(last - acs) * dt
    chunk_decay = jnp.broadcast_to(jnp.exp(last), acs.shape)
    if not first:
        acs_col = jnp.concatenate([acs, jnp.zeros((BLOCK - SSD_HEADS, BLOCK), F32)], axis=0).T
        grow = jnp.exp(acs)
        ssq = jnp.zeros((1, BLOCK), F32)

    for g in range(SSD_GROUPS):
        b_rows = slice(SSD_INNER + g * SSD_STATE, SSD_INNER + (g + 1) * SSD_STATE)
        c_rows = slice(SSD_INNER + SSD_BC + g * SSD_STATE, SSD_INNER + SSD_BC + (g + 1) * SSD_STATE)
        bg = xbc_ref[0, b_rows, ts].astype(BF16)
        heads = range(g * HEADS_PER_GROUP, (g + 1) * HEADS_PER_GROUP)
        grp = slice(g * GROUP_INNER, (g + 1) * GROUP_INNER)
        per_row = lambda a: jnp.concatenate(
            [jnp.broadcast_to(a[h:h + 1, :], (SSD_HEADDIM, BLOCK)) for h in heads], axis=0)
        inject = _dot_nt((xbc_ref[0, grp, ts] * per_row(to_end)).astype(BF16), bg)
        if first:
            state_ref[grp, :] = inject
            continue
        cg = xbc_ref[0, c_rows, ts].astype(BF16)
        cb_t = _dot_tn(bg, cg)
        st = state_ref[grp, :]
        state_ref[grp, :] = st * per_row(chunk_decay) + inject
        y_off = _dot(st.astype(BF16), cg)
        for e, h in enumerate(heads):
            rows = slice(h * SSD_HEADDIM, (h + 1) * SSD_HEADDIM)
            xs = xbc_ref[0, rows, ts]
            seg = acs[h:h + 1, :] - acs_col[:, h:h + 1]
            g_t = (cb_t * jnp.where(upper, jnp.exp(seg), 0.0)).astype(BF16)
            y = _dot((xs * dt[h:h + 1, :]).astype(BF16), g_t)
            y = (y + y_off[e * SSD_HEADDIM:(e + 1) * SSD_HEADDIM] * grow[h:h + 1, :]
                 + xs * dskip_ref[rows, :])
            y = y * _silu(zt_ref[0, rows, ts].astype(F32))
            ssq = ssq + jnp.sum(y * y, axis=0, keepdims=True)
            ygate_ref[rows, :] = y

    if not first:
        scale = lax.rsqrt(ssq * (1.0 / SSD_INNER) + NORM_EPS)
        for blk in range(SSD_INNER // BLOCK):
            rs = slice(blk * BLOCK, (blk + 1) * BLOCK)
            y = ygate_ref[rs, :] * scale * ng_ref[rs, :]
            y_ref[0, ts, rs] = y.T.astype(y_ref.dtype)


def _mixer_kernel(q_ref, k_ref, vt_ref, k_meta_ref, vt_meta_ref, slope_ref, coef_ref, pos_ref,
                  lam_ref, g_ref, xbct_ref, dtt_ref, zt_ref, xbct_meta_ref, dtt_meta_ref,
                  dtb_ref, alog_ref, dskip_ref, ng_ref, o_ref, y_ref, qa_ref, m_ref, acc_ref,
                  sa_ref, ca_ref, sb_ref, cb_ref, state_ref, ygate_ref, *, lam_init, n_tiles):
    j = pl.program_id(2)
    scan = functools.partial(_ssd_scan, zt_ref=zt_ref, y_ref=y_ref, dtb_ref=dtb_ref,
                             alog_ref=alog_ref, dskip_ref=dskip_ref, ng_ref=ng_ref,
                             state_ref=state_ref, ygate_ref=ygate_ref)

    @pl.when((pl.program_id(1) == 0) & (j == 0))
    def _():
        scan(xbct_meta_ref, dtt_meta_ref, slice(0, BLOCK), first=True)

    t = ATTN_TILE
    hc = ATT_HEAD_COLS
    group = ATTN_GROUP
    steps = n_tiles // group
    q_tiles = []
    for pair in range(group // 2):
        q_tiles += [j + pair * steps, n_tiles - 1 - j - pair * steps]
    slope = slope_ref[0][:, 0:1]
    key_row = lax.broadcasted_iota(jnp.int32, (t, t), 0)
    qry_col = lax.broadcasted_iota(jnp.int32, (t, t), 1)
    meta_row = lax.broadcasted_iota(jnp.int32, (BLOCK, t), 0)

    def rows(tile):
        return pl.ds(pl.multiple_of(tile * t, t), t)

    coef = jnp.broadcast_to(coef_ref[0], (t, LANES)).astype(BF16)
    lane = lax.broadcasted_iota(jnp.int32, (t, hc), 1)
    for w, tile in enumerate(q_tiles):
        qf = q_ref[0, rows(tile), :]
        for comp, keep in enumerate((lane < ATT_HEAD_DIM, lane >= ATT_HEAD_DIM)):
            qa_ref[w, comp] = jnp.concatenate([jnp.where(keep, qf, jnp.zeros_like(qf)), coef],
                                              axis=1)
    m_ref[...] = jnp.full(m_ref.shape, NEG, F32)
    acc_ref[...] = jnp.zeros(acc_ref.shape, F32)

    def scores(kt, w, mask):
        k_aug = jnp.concatenate([kt, pos_ref[0:kt.shape[0], :]], axis=1)
        out = []
        for comp in range(2):
            s = _dot_nt(k_aug, qa_ref[w, comp])
            out.append(s if mask is None else jnp.where(mask, s, NEG))
        return out

    def fold(w, s_of, cmax_of, vtt, key_start):
        sigma = slope * lax.convert_element_type(key_start, F32)
        v_aug = jnp.concatenate([vtt, jnp.ones((DENOM_ROWS, vtt.shape[1]), BF16)], axis=0)
        for comp in range(2):
            m_old = m_ref[w, comp]
            m_new = jnp.maximum(m_old, cmax_of(comp) + sigma)
            alpha = jnp.exp2(m_old - m_new)
            pr = jnp.exp2(s_of(comp) - (m_new - sigma))
            acc_ref[w, comp] = alpha * acc_ref[w, comp] + _dot(v_aug, pr.astype(BF16))
            m_ref[w, comp] = m_new

    n_full = (group // 2) * (n_tiles - 1)
    n_items = group + n_full
    lead = (n_items - 1) % ATTN_RUN
    assert ATTN_RUN % 2 == 0 and group - 1 <= ATTN_RUN and n_items - 1 >= ATTN_RUN + lead
    starts = [sum(q_tiles[:w]) for w in range(group)]

    def pick(w, values):
        out = values[0]
        for v in range(1, group):
            out = jnp.where(w == v, values[v], out)
        return out

    def item(r):
        r = jnp.asarray(r, jnp.int32)
        w_full = sum((r - 1 >= starts[v]).astype(jnp.int32) for v in range(1, group))
        w = jnp.where(r == 0, 0, jnp.where(r > n_full, r - n_full, w_full))
        on_diagonal = (r == 0) | (r > n_full)
        key_tile = jnp.where(on_diagonal, pick(w, q_tiles), r - 1 - pick(w_full, starts))
        return w, key_tile, pick(w, q_tiles)

    slots = ((sa_ref, ca_ref), (sb_ref, cb_ref))

    def stage(r, parity, masked):
        w, key_tile, _ = item(r)
        stage_ref, cmax_ref = slots[parity]
        mask = (key_row <= qry_col) if masked else None
        for comp, s in enumerate(scores(k_ref[0, rows(key_tile), :], w, mask)):
            stage_ref[comp] = s
            cmax_ref[comp] = jnp.max(s, axis=0, keepdims=True)

    def fold_staged(r, parity):
        w, key_tile, q_tile = item(r)
        stage_ref, cmax_ref = slots[parity]
        fold(w, lambda comp: stage_ref[comp], lambda comp: cmax_ref[comp],
             vt_ref[0, :, rows(key_tile)], (key_tile - q_tile) * t)

    s_meta = [scores(k_meta_ref[0], w, meta_row >= PAD_FRONT) for w in range(group)]
    stage(0, 0, True)
    for w, tile in enumerate(q_tiles):
        fold(w, lambda comp: s_meta[w][comp],
             lambda comp: jnp.max(s_meta[w][comp], axis=0, keepdims=True),
             vt_meta_ref[0], -(BLOCK + tile * t))

    def steps_from(base, base_parity, count):
        for c in range(1, count + 1):
            stage(base + c, (base_parity + c) % 2, isinstance(base, int) and base + c > n_full)
            fold_staged(base + c - 1, (base_parity + c - 1) % 2)

    def run_body(u, carry):
        steps_from(lead + u * ATTN_RUN, lead % 2, ATTN_RUN)
        return carry

    if lead:
        steps_from(0, 0, lead)
    n_runs = (n_items - 1 - lead) // ATTN_RUN
    lax.fori_loop(0, n_runs - 1, run_body, 0)
    for sub in range(group // 2):
        scan(xbct_ref, dtt_ref, slice(sub * BLOCK, (sub + 1) * BLOCK), first=False)
    steps_from(lead + (n_runs - 1) * ATTN_RUN, lead % 2, ATTN_RUN)
    fold_staged(n_items - 1, (n_items - 1) % 2)

    lam = (jnp.exp(jnp.sum(lam_ref[0:1] * lam_ref[1:2], axis=-1, keepdims=True))
           - jnp.exp(jnp.sum(lam_ref[2:3] * lam_ref[3:4], axis=-1, keepdims=True)) + lam_init)
    for w, tile in enumerate(q_tiles):
        out = [acc_ref[w, comp, 0:hc, :] * (1.0 / acc_ref[w, comp, hc:hc + 1, :])
               for comp in range(2)]
        o = out[0] - lam * out[1]
        ms = jnp.mean(o * o, axis=0, keepdims=True)
        y = o * lax.rsqrt(ms + NORM_EPS) * g_ref[...] * (1.0 - lam_init)
        o_ref[0, rows(tile), :] = y.T.astype(o_ref.dtype)


def _alibi_operands():
    heads = np.arange(ATT_HEADS, dtype=np.float32)
    slope = (2.0 ** (-8.0 * (heads + 1.0) / ATT_HEADS)).astype(np.float32) * np.float32(LOG2E)
    pieces, rest = [], slope
    for _ in range(3):
        piece = rest.astype(BF16).astype(np.float32)
        pieces.append(piece)
        rest = rest - piece
    cols = [np.float32(POS_RADIX) * p for p in pieces] + pieces
    coef = np.zeros((ATT_HEADS, 1, LANES), np.float32)
    coef[:, 0, :len(cols)] = np.stack(cols, axis=1)
    row = np.arange(ATTN_TILE)
    pos = np.zeros((ATTN_TILE, LANES), np.float32)
    pos[:, 0:3] = (row // POS_RADIX)[:, None]
    pos[:, 3:6] = (row % POS_RADIX)[:, None]
    slope = np.broadcast_to(slope[:, None, None], (ATT_HEADS, 1, LANES))
    return jnp.asarray(slope), jnp.asarray(coef), jnp.asarray(pos, dtype=BF16)


def _mixers(q, k, vt, k_meta, vt_meta, lam_vecs, subln_g, xbct, dtt, zt, xbct_meta, dtt_meta,
            ssd_params, *, lam_init):
    nb, seq, _ = q.shape
    t = ATTN_TILE
    hc = ATT_HEAD_COLS
    n_tiles = seq // t
    group = ATTN_GROUP
    steps = n_tiles // group
    span = BLOCK * group // 2
    assert n_tiles % group == 0 and ATT_HEADS * steps * span == seq
    slopes, coef, pos = _alibi_operands()
    per_head_rows = pl.BlockSpec((1, seq, hc), lambda b, h, j: (b, 0, h))
    chunk = lambda rows: pl.BlockSpec((1, rows, span), lambda b, h, j: (b, 0, h * steps + j))
    return pl.pallas_call(
        functools.partial(_mixer_kernel, lam_init=lam_init, n_tiles=n_tiles),
        grid=(nb, ATT_HEADS, steps),
        in_specs=[per_head_rows, per_head_rows,
                  pl.BlockSpec((1, hc, seq), lambda b, h, j: (b, h, 0)),
                  pl.BlockSpec((1, BLOCK, hc), lambda b, h, j: (0, 0, h)),
                  pl.BlockSpec((1, hc, BLOCK), lambda b, h, j: (0, h, 0)),
                  pl.BlockSpec((1, 1, LANES), lambda b, h, j: (h, 0, 0)),
                  pl.BlockSpec((1, 1, LANES), lambda b, h, j: (h, 0, 0)),
                  _resident(pos.shape), _resident(lam_vecs.shape), _resident(subln_g.shape),
                  chunk(SSD_CONV_DIM), chunk(SSD_HEADS), chunk(SSD_INNER),
                  _resident(xbct_meta.shape), _resident(dtt_meta.shape)]
                 + [_resident(a.shape) for a in ssd_params],
        out_specs=[per_head_rows,
                   pl.BlockSpec((1, span, SSD_INNER), lambda b, h, j: (b, h * steps + j, 0))],
        out_shape=[jax.ShapeDtypeStruct((nb, seq, ATT_V), BF16),
                   jax.ShapeDtypeStruct((nb, seq, SSD_INNER), BF16)],
        scratch_shapes=[pltpu.VMEM((group, 2, t, 2 * LANES), BF16),
                        pltpu.VMEM((group, 2, 1, t), F32),
                        pltpu.VMEM((group, 2, hc + DENOM_ROWS, t), F32)]
                       + [pltpu.VMEM((2, t, t), F32), pltpu.VMEM((2, 1, t), F32)] * 2
                       + [pltpu.VMEM((SSD_INNER, SSD_STATE), F32),
                          pltpu.VMEM((SSD_INNER, BLOCK), F32)],
        compiler_params=pltpu.CompilerParams(
            dimension_semantics=("parallel", "arbitrary", "arbitrary"),
            vmem_limit_bytes=VMEM_LIMIT_BYTES),
        name="mixers",
    )(q, k, vt, k_meta, vt_meta, slopes, coef, pos, lam_vecs, subln_g, xbct, dtt, zt,
      xbct_meta, dtt_meta, *ssd_params)


def _merge_ffn_kernel(x_ref, ys_ref, ya_ref, gates_ref, gbias_ref, ws_ref, wa_ref, wo_ref,
                      nffn_ref, wg_ref, wu_ref, wd_ref, nfin_ref, o_ref):
    d = x_ref.shape[-1]
    gates = _sigmoid(gates_ref[...].astype(F32) + gbias_ref[...])
    merged = (gates[:, :d] * _dot(ys_ref[...], ws_ref[...])
              + gates[:, d:] * _dot(ya_ref[...], wa_ref[...]))
    h = x_ref[...] + _dot(merged.astype(BF16), wo_ref[...])
    u = h * lax.rsqrt(jnp.mean(h * h, axis=-1, keepdims=True) + NORM_EPS) * nffn_ref[...]
    ub = u.astype(BF16)
    hidden = _silu(_dot(ub, wg_ref[...])) * _dot(ub, wu_ref[...])
    h = h + _dot(hidden.astype(BF16), wd_ref[...])
    o_ref[...] = h * lax.rsqrt(jnp.mean(h * h, axis=-1, keepdims=True) + NORM_EPS) * nfin_ref[...]


def _merge_ffn(x2, ys, ya, gates, consts):
    n, d = x2.shape
    rows = FFN_ROWS
    tile = lambda cols: pl.BlockSpec((rows, cols), lambda i: (i, 0))
    gbias, ws, wa, wo, nffn, wg, wu, wd, nfin = consts
    return pl.pallas_call(
        _merge_ffn_kernel,
        grid=(n // rows,),
        in_specs=[tile(d), tile(ys.shape[1]), tile(ya.shape[1]), tile(gates.shape[1])]
                 + [_resident(a.shape) for a in consts],
        out_specs=tile(d),
        out_shape=jax.ShapeDtypeStruct((n, d), F32),
        compiler_params=pltpu.CompilerParams(
            dimension_semantics=("parallel",), vmem_limit_bytes=VMEM_LIMIT_BYTES),
        name="merge_ffn",
    )(x2, ys, ya, gates, *consts)


def _lane_bcast(v, n=LANES):
    return jnp.broadcast_to(v.astype(F32)[:, None], (v.shape[0], n))


def _layer(x, meta_chunk, l, norm_mix_g, w_in, gate_bias, conv_w, conv_b, dt_bias, a_log, d_skip,
           ssd_norm_g, lam_vecs, subln_g, w_ssd_branch, w_attn_branch, w_out, norm_ffn_g,
           w_gate_ffn, w_up_ffn, w_down_ffn, norm_final_g):
    nb, seq, d = x.shape
    sizes = [SSD_INNER, SSD_CONV_DIM, SSD_HEADS, ATT_QK, ATT_QK, ATT_V, 2 * d]
    offs = [0]
    for s in sizes:
        offs.append(offs[-1] + s)
    col = lambda idx: w_in[:, offs[idx]:offs[idx + 1]].astype(BF16)
    w = {"z_t": col(0).T, "xbc_t": col(1).T, "dt_t": col(2).T, "q": col(3), "k": col(4),
         "v_t": col(5).T, "gates": col(6)}
    g_mix = norm_mix_g.astype(F32)[None, :]

    conv_taps = jnp.stack([_lane_bcast(conv_w[:, t]) for t in range(SSD_CONV)])
    conv_bias = _lane_bcast(conv_b)
    _, k_meta, _, xbct_meta, _, vt_meta, dtt_meta, meta_tail = _in_projection(
        meta_chunk, g_mix, w, conv_taps, conv_bias, jnp.zeros((1, SSD_CONV_DIM, BLOCK), F32),
        rows=BLOCK, masked_rows=PAD_FRONT)
    q, k, gates, xbct, zt, vt, dtt, _ = _in_projection(
        x, g_mix, w, conv_taps, conv_bias, meta_tail, rows=INPROJ_ROWS, masked_rows=0)

    ssd_params = [_lane_bcast(dt_bias), _lane_bcast(a_log),
                  _lane_bcast(jnp.repeat(d_skip, SSD_HEADDIM)), _lane_bcast(ssd_norm_g)]
    lam_init = 0.8 - 0.6 * math.exp(-0.3 * l)
    y_att, y_ssd = _mixers(q, k, vt, k_meta, vt_meta, lam_vecs.astype(F32),
                           _lane_bcast(subln_g, ATTN_TILE), xbct, dtt, zt, xbct_meta, dtt_meta,
                           ssd_params, lam_init=lam_init)

    consts = [gate_bias.astype(F32)[None, :], w_ssd_branch.astype(BF16),
              w_attn_branch.astype(BF16), w_out.astype(BF16), norm_ffn_g.astype(F32)[None, :],
              w_gate_ffn.astype(BF16), w_up_ffn.astype(BF16), w_down_ffn.astype(BF16),
              norm_final_g.astype(F32)[None, :]]
    out = _merge_ffn(x.reshape(nb * seq, d), y_ssd.reshape(nb * seq, SSD_INNER),
                     y_att.reshape(nb * seq, ATT_V), gates.reshape(nb * seq, 2 * d), consts)
    return out.reshape(nb, seq, d)


def kernel(x, meta_tokens, norm_mix_g, w_in, gate_bias, conv_w, conv_b, dt_bias, a_log, d_skip, ssd_norm_g, lambda_q1, lambda_k1, lambda_q2, lambda_k2, subln_g, w_ssd_branch, w_attn_branch, w_out, norm_ffn_g, w_gate_ffn, w_up_ffn, w_down_ffn, norm_final_g):
    depth = w_in.shape[0]
    assert depth == 1, "the fused final norm assumes a single layer"
    d = x.shape[-1]
    meta_chunk = jnp.concatenate(
        [jnp.zeros((PAD_FRONT, d), x.dtype), meta_tokens.astype(x.dtype)], axis=0)[None]
    l = 0
    lam_vecs = jnp.stack([lambda_q1[l], lambda_k1[l], lambda_q2[l], lambda_k2[l]])
    return _layer(x, meta_chunk, l, norm_mix_g[l], w_in[l], gate_bias[l], conv_w[l], conv_b[l],
                  dt_bias[l], a_log[l], d_skip[l], ssd_norm_g[l], lam_vecs, subln_g[l],
                  w_ssd_branch[l], w_attn_branch[l], w_out[l], norm_ffn_g[l], w_gate_ffn[l],
                  w_up_ffn[l], w_down_ffn[l], norm_final_g)
```

```python
import functools
import math

import jax
import jax.numpy as jnp
import numpy as np
from jax import lax
from jax.experimental import pallas as pl
from jax.experimental.pallas import tpu as pltpu

F32 = jnp.float32
BF16 = jnp.bfloat16

N_META = 16
BLOCK = 128
PAD_FRONT = BLOCK - N_META
NORM_EPS = 1e-6
NEG = -1e30

SSD_HEADDIM = 64
SSD_HEADS = 16
SSD_INNER = SSD_HEADS * SSD_HEADDIM
SSD_GROUPS = 2
SSD_STATE = 128
SSD_CONV = 4
SSD_BC = SSD_GROUPS * SSD_STATE
SSD_CONV_DIM = SSD_INNER + 2 * SSD_BC
HEADS_PER_GROUP = SSD_HEADS // SSD_GROUPS
GROUP_INNER = HEADS_PER_GROUP * SSD_HEADDIM

ATT_HEADS = 8
ATT_HEAD_DIM = 64
ATT_QK = ATT_HEADS * 2 * ATT_HEAD_DIM
ATT_V = ATT_HEADS * 2 * ATT_HEAD_DIM
ATT_HEAD_COLS = 2 * ATT_HEAD_DIM

LOG2E = math.log2(math.e)
Q_SCALE = ATT_HEAD_DIM ** -0.5 * LOG2E

LANES = 128
VMEM_LIMIT_BYTES = 56 * 1024 * 1024
INPROJ_ROWS = 512
CONV_ROWS = 64
ATTN_TILE = 512
ATTN_RUN = 8
ATTN_GROUP = 4
POS_RADIX = 16
DENOM_ROWS = 16
FFN_ROWS = 512


def _resident(shape):
    zeros = (0,) * len(shape)
    return pl.BlockSpec(shape, lambda *_: zeros, pipeline_mode=pl.Buffered(1))


def _sigmoid(x):
    return 1.0 / (1.0 + jnp.exp2(x * (-LOG2E)))


def _silu(x):
    return x * _sigmoid(x)


def _dot(a, b):
    return jnp.dot(a, b, preferred_element_type=F32)


def _dot_nt(a, b):
    return lax.dot_general(a, b, (((1,), (1,)), ((), ())), preferred_element_type=F32)


def _dot_tn(a, b):
    return lax.dot_general(a, b, (((0,), (0,)), ((), ())), preferred_element_type=F32)


def _inproj_kernel(x_ref, g_ref, wq_ref, wk_ref, wg_ref, wxbct_ref, wzt_ref, wvt_ref, wdtt_ref,
                   convw_ref, convb_ref, hist_ref, q_ref, k_ref, gates_ref, xbct_ref, zt_ref,
                   vt_ref, dtt_ref, tail_ref, raw_ref, *, masked_rows):
    rows = x_ref.shape[1]
    x = x_ref[0]
    ms = jnp.mean(x * x, axis=-1, keepdims=True)
    u = x * lax.rsqrt(ms + NORM_EPS) * g_ref[...]
    if masked_rows:
        row = lax.broadcasted_iota(jnp.int32, u.shape, 0)
        u = jnp.where(row >= masked_rows, u, 0.0)
    ub = u.astype(BF16)

    @pl.when(pl.program_id(1) == 0)
    def _():
        raw_ref[:, 0:BLOCK] = hist_ref[0]

    raw_ref[:, BLOCK:] = _dot_nt(wxbct_ref[...], ub)
    q_ref[0] = (_dot(ub, wq_ref[...]) * Q_SCALE).astype(q_ref.dtype)
    k_ref[0] = _dot(ub, wk_ref[...]).astype(k_ref.dtype)
    gates_ref[0] = _dot(ub, wg_ref[...]).astype(gates_ref.dtype)
    zt_ref[0] = _dot_nt(wzt_ref[...], ub).astype(zt_ref.dtype)
    vt_ref[0] = _dot_nt(wvt_ref[...], ub).astype(vt_ref.dtype)
    dtt_ref[0] = _dot_nt(wdtt_ref[...], ub)
    lane = lax.broadcasted_iota(jnp.int32, (CONV_ROWS, BLOCK), 1)
    for blk in range(SSD_CONV_DIM // CONV_ROWS):
        rs = slice(blk * CONV_ROWS, (blk + 1) * CONV_ROWS)
        taps = [convw_ref[t, rs, :] for t in range(SSD_CONV)]
        bias = convb_ref[rs, :]
        for cb in range(rows // BLOCK):
            x_prev = raw_ref[rs, cb * BLOCK:(cb + 1) * BLOCK]
            xc = raw_ref[rs, (cb + 1) * BLOCK:(cb + 2) * BLOCK]
            acc = bias + taps[SSD_CONV - 1] * xc
            for j in range(1, SSD_CONV):
                joined = jnp.where(lane >= BLOCK - j, x_prev, xc)
                acc = acc + taps[SSD_CONV - 1 - j] * pltpu.roll(joined, j, 1)
            xbct_ref[0, rs, cb * BLOCK:(cb + 1) * BLOCK] = _silu(acc)
        tail = raw_ref[rs, rows:rows + BLOCK]
        raw_ref[rs, 0:BLOCK] = tail
        tail_ref[0, rs, :] = tail


def _in_projection(h, g, w, conv_w, conv_b, hist, *, rows, masked_rows):
    nb, seq, d = h.shape
    grid = (nb, seq // rows)
    row_major = lambda cols, dt: (jax.ShapeDtypeStruct((nb, seq, cols), dt),
                                  pl.BlockSpec((1, rows, cols), lambda b, i: (b, i, 0)))
    chan_major = lambda cols, dt: (jax.ShapeDtypeStruct((nb, cols, seq), dt),
                                   pl.BlockSpec((1, cols, rows), lambda b, i: (b, 0, i)))
    tail = (jax.ShapeDtypeStruct((nb, SSD_CONV_DIM, BLOCK), F32),
            pl.BlockSpec((1, SSD_CONV_DIM, BLOCK), lambda b, i: (b, 0, 0)))
    outs = [row_major(ATT_QK, BF16), row_major(ATT_QK, BF16), row_major(2 * d, BF16),
            chan_major(SSD_CONV_DIM, F32), chan_major(SSD_INNER, BF16), chan_major(ATT_V, BF16),
            chan_major(SSD_HEADS, F32), tail]
    consts = [g, w["q"], w["k"], w["gates"], w["xbc_t"], w["z_t"], w["v_t"], w["dt_t"],
              conv_w, conv_b, hist]
    return pl.pallas_call(
        functools.partial(_inproj_kernel, masked_rows=masked_rows),
        grid=grid,
        in_specs=[pl.BlockSpec((1, rows, d), lambda b, i: (b, i, 0))]
                 + [_resident(a.shape) for a in consts],
        out_specs=[o[1] for o in outs],
        out_shape=[o[0] for o in outs],
        scratch_shapes=[pltpu.VMEM((SSD_CONV_DIM, BLOCK + rows), F32)],
        compiler_params=pltpu.CompilerParams(
            dimension_semantics=("parallel", "arbitrary"), vmem_limit_bytes=VMEM_LIMIT_BYTES),
        name="in_projection",
    )(h, *consts)


def _ssd_scan(xbc_ref, dtt_ref, ts, zt_ref, y_ref, dtb_ref, alog_ref, dskip_ref, ng_ref,
              state_ref, ygate_ref, *, first):
    v = dtt_ref[0, :, ts] + dtb_ref[...]
    dt = jnp.maximum(v, 0.0) + jnp.log1p(jnp.exp(-jnp.abs(v)))
    if first:
        step = lax.broadcasted_iota(jnp.int32, dt.shape, 1)
        dt = jnp.where(step >= PAD_FRONT, dt, 0.0)
    da = dt * (-jnp.exp(alog_ref[...]))
    r = lax.broadcasted_iota(jnp.int32, (BLOCK, BLOCK), 0)
    c = lax.broadcasted_iota(jnp.int32, (BLOCK, BLOCK), 1)
    upper = r <= c
    acs = jnp.dot(da, upper.astype(F32), preferred_element_type=F32,
                  precision=lax.Precision.HIGHEST)
    last = acs[:, BLOCK - 1:BLOCK]
    to_end = jnp.exp(last - acs) * dt
    chunk_decay = jnp.broadcast_to(jnp.exp(last), acs.shape)
    if not first:
        acs_col = jnp.concatenate([acs, jnp.zeros((BLOCK - SSD_HEADS, BLOCK), F32)], axis=0).T
        grow = jnp.exp(acs)
        ssq = jnp.zeros((1, BLOCK), F32)

    for g in range(SSD_GROUPS):
        b_rows = slice(SSD_INNER + g * SSD_STATE, SSD_INNER + (g + 1) * SSD_STATE)
        c_rows = slice(SSD_INNER + SSD_BC + g * SSD_STATE, SSD_INNER + SSD_BC + (g + 1) * SSD_STATE)
        bg = xbc_ref[0, b_rows, ts].astype(BF16)
        heads = range(g * HEADS_PER_GROUP, (g + 1) * HEADS_PER_GROUP)
        grp = slice(g * GROUP_INNER, (g + 1) * GROUP_INNER)
        per_row = lambda a: jnp.concatenate(
            [jnp.broadcast_to(a[h:h + 1, :], (SSD_HEADDIM, BLOCK)) for h in heads], axis=0)
        inject = _dot_nt((xbc_ref[0, grp, ts] * per_row(to_end)).astype(BF16), bg)
        if first:
            state_ref[grp, :] = inject
            continue
        cg = xbc_ref[0, c_rows, ts].astype(BF16)
        cb_t = _dot_tn(bg, cg)
        st = state_ref[grp, :]
        state_ref[grp, :] = st * per_row(chunk_decay) + inject
        y_off = _dot(st.astype(BF16), cg)
        for e, h in enumerate(heads):
            rows = slice(h * SSD_HEADDIM, (h + 1) * SSD_HEADDIM)
            xs = xbc_ref[0, rows, ts]
            seg = acs[h:h + 1, :] - acs_col[:, h:h + 1]
            g_t = (cb_t * jnp.where(upper, jnp.exp(seg), 0.0)).astype(BF16)
            y = _dot((xs * dt[h:h + 1, :]).astype(BF16), g_t)
            y = (y + y_off[e * SSD_HEADDIM:(e + 1) * SSD_HEADDIM] * grow[h:h + 1, :]
                 + xs * dskip_ref[rows, :])
            y = y * _silu(zt_ref[0, rows, ts].astype(F32))
            ssq = ssq + jnp.sum(y * y, axis=0, keepdims=True)
            ygate_ref[rows, :] = y

    if not first:
        scale = lax.rsqrt(ssq * (1.0 / SSD_INNER) + NORM_EPS)
        for blk in range(SSD_INNER // BLOCK):
            rs = slice(blk * BLOCK, (blk + 1) * BLOCK)
            y = ygate_ref[rs, :] * scale * ng_ref[rs, :]
            y_ref[0, ts, rs] = y.T.astype(y_ref.dtype)


def _mixer_kernel(q_ref, k_ref, vt_ref, k_meta_ref, vt_meta_ref, slope_ref, coef_ref, pos_ref,
                  lam_ref, g_ref, xbct_ref, dtt_ref, zt_ref, xbct_meta_ref, dtt_meta_ref,
                  dtb_ref, alog_ref, dskip_ref, ng_ref, o_ref, y_ref, qa_ref, m_ref, acc_ref,
                  sa_ref, ca_ref, sb_ref, cb_ref, state_ref, ygate_ref, *, lam_init, n_tiles):
    j = pl.program_id(2)
    scan = functools.partial(_ssd_scan, zt_ref=zt_ref, y_ref=y_ref, dtb_ref=dtb_ref,
                             alog_ref=alog_ref, dskip_ref=dskip_ref, ng_ref=ng_ref,
                             state_ref=state_ref, ygate_ref=ygate_ref)

    @pl.when((pl.program_id(1) == 0) & (j == 0))
    def _():
        scan(xbct_meta_ref, dtt_meta_ref, slice(0, BLOCK), first=True)

    t = ATTN_TILE
    hc = ATT_HEAD_COLS
    group = ATTN_GROUP
    steps = n_tiles // group
    q_tiles = []
    for pair in range(group // 2):
        q_tiles += [j + pair * steps, n_tiles - 1 - j - pair * steps]
    slope = slope_ref[0][:, 0:1]
    key_row = lax.broadcasted_iota(jnp.int32, (t, t), 0)
    qry_col = lax.broadcasted_iota(jnp.int32, (t, t), 1)
    meta_row = lax.broadcasted_iota(jnp.int32, (BLOCK, t), 0)

    def rows(tile):
        return pl.ds(pl.multiple_of(tile * t, t), t)

    coef = coef_ref[0]
    chan = lax.broadcasted_iota(jnp.int32, (hc, t), 0)
    for w, tile in enumerate(q_tiles):
        q_t = q_ref[0, rows(tile), :].astype(F32).T
        for comp, keep in enumerate((chan < ATT_HEAD_DIM, chan >= ATT_HEAD_DIM)):
            qa_ref[w, comp] = jnp.concatenate(
                [jnp.where(keep, q_t, 0.0).astype(BF16), coef], axis=0)
    m_ref[...] = jnp.full(m_ref.shape, NEG, F32)
    acc_ref[...] = jnp.zeros(acc_ref.shape, F32)

    def scores(kt, w, mask):
        k_aug = jnp.concatenate([kt, pos_ref[0:kt.shape[0], :]], axis=1)
        out = []
        for comp in range(2):
            s = _dot(k_aug, qa_ref[w, comp])
            out.append(s if mask is None else jnp.where(mask, s, NEG))
        return out

    def fold(w, s_of, cmax_of, vtt, key_start):
        sigma = slope * lax.convert_element_type(key_start, F32)
        v_aug = jnp.concatenate([vtt, jnp.ones((DENOM_ROWS, vtt.shape[1]), BF16)], axis=0)
        for comp in range(2):
            m_old = m_ref[w, comp]
            m_new = jnp.maximum(m_old, cmax_of(comp) + sigma)
            alpha = jnp.exp2(m_old - m_new)
            pr = jnp.exp2(s_of(comp) - (m_new - sigma))
            acc_ref[w, comp] = alpha * acc_ref[w, comp] + _dot(v_aug, pr.astype(BF16))
            m_ref[w, comp] = m_new

    n_full = (group // 2) * (n_tiles - 1)
    n_items = group + n_full
    lead = (n_items - 1) % ATTN_RUN
    assert ATTN_RUN % 2 == 0 and group - 1 <= ATTN_RUN and n_items - 1 >= ATTN_RUN + lead
    starts = [sum(q_tiles[:w]) for w in range(group)]

    def pick(w, values):
        out = values[0]
        for v in range(1, group):
            out = jnp.where(w == v, values[v], out)
        return out

    def item(r):
        r = jnp.asarray(r, jnp.int32)
        w_full = sum((r - 1 >= starts[v]).astype(jnp.int32) for v in range(1, group))
        w = jnp.where(r == 0, 0, jnp.where(r > n_full, r - n_full, w_full))
        on_diagonal = (r == 0) | (r > n_full)
        key_tile = jnp.where(on_diagonal, pick(w, q_tiles), r - 1 - pick(w_full, starts))
        return w, key_tile, pick(w, q_tiles)

    slots = ((sa_ref, ca_ref), (sb_ref, cb_ref))

    def stage(r, parity, masked):
        w, key_tile, _ = item(r)
        stage_ref, cmax_ref = slots[parity]
        mask = (key_row <= qry_col) if masked else None
        for comp, s in enumerate(scores(k_ref[0, rows(key_tile), :], w, mask)):
            stage_ref[comp] = s
            cmax_ref[comp] = jnp.max(s, axis=0, keepdims=True)

    def fold_staged(r, parity):
        w, key_tile, q_tile = item(r)
        stage_ref, cmax_ref = slots[parity]
        fold(w, lambda comp: stage_ref[comp], lambda comp: cmax_ref[comp],
             vt_ref[0, :, rows(key_tile)], (key_tile - q_tile) * t)

    s_meta = [scores(k_meta_ref[0], w, meta_row >= PAD_FRONT) for w in range(group)]
    stage(0, 0, True)
    for w, tile in enumerate(q_tiles):
        fold(w, lambda comp: s_meta[w][comp],
             lambda comp: jnp.max(s_meta[w][comp], axis=0, keepdims=True),
             vt_meta_ref[0], -(BLOCK + tile * t))

    def steps_from(base, base_parity, count):
        for c in range(1, count + 1):
            stage(base + c, (base_parity + c) % 2, isinstance(base, int) and base + c > n_full)
            fold_staged(base + c - 1, (base_parity + c - 1) % 2)

    def run_body(u, carry):
        steps_from(lead + u * ATTN_RUN, lead % 2, ATTN_RUN)
        return carry

    if lead:
        steps_from(0, 0, lead)
    n_runs = (n_items - 1 - lead) // ATTN_RUN
    lax.fori_loop(0, n_runs - 1, run_body, 0)
    for sub in range(group // 2):
        scan(xbct_ref, dtt_ref, slice(sub * BLOCK, (sub + 1) * BLOCK), first=False)
    steps_from(lead + (n_runs - 1) * ATTN_RUN, lead % 2, ATTN_RUN)
    fold_staged(n_items - 1, (n_items - 1) % 2)

    lam = (jnp.exp(jnp.sum(lam_ref[0:1] * lam_ref[1:2], axis=-1, keepdims=True))
           - jnp.exp(jnp.sum(lam_ref[2:3] * lam_ref[3:4], axis=-1, keepdims=True)) + lam_init)
    for w, tile in enumerate(q_tiles):
        out = [acc_ref[w, comp, 0:hc, :] * (1.0 / acc_ref[w, comp, hc:hc + 1, :])
               for comp in range(2)]
        o = out[0] - lam * out[1]
        ms = jnp.mean(o * o, axis=0, keepdims=True)
        y = o * lax.rsqrt(ms + NORM_EPS) * g_ref[...] * (1.0 - lam_init)
        o_ref[0, rows(tile), :] = y.T.astype(o_ref.dtype)


def _alibi_operands():
    heads = np.arange(ATT_HEADS, dtype=np.float32)
    slope = (2.0 ** (-8.0 * (heads + 1.0) / ATT_HEADS)).astype(np.float32) * np.float32(LOG2E)
    pieces, rest = [], slope
    for _ in range(3):
        piece = rest.astype(BF16).astype(np.float32)
        pieces.append(piece)
        rest = rest - piece
    cols = [np.float32(POS_RADIX) * p for p in pieces] + pieces
    coef = np.zeros((ATT_HEADS, LANES, ATTN_TILE), np.float32)
    coef[:, :len(cols), :] = np.stack(cols, axis=1)[:, :, None]
    row = np.arange(ATTN_TILE)
    pos = np.zeros((ATTN_TILE, LANES), np.float32)
    pos[:, 0:3] = (row // POS_RADIX)[:, None]
    pos[:, 3:6] = (row % POS_RADIX)[:, None]
    slope = np.broadcast_to(slope[:, None, None], (ATT_HEADS, 1, LANES))
    return jnp.asarray(slope), jnp.asarray(coef, dtype=BF16), jnp.asarray(pos, dtype=BF16)


def _mixers(q, k, vt, k_meta, vt_meta, lam_vecs, subln_g, xbct, dtt, zt, xbct_meta, dtt_meta,
            ssd_params, *, lam_init):
    nb, seq, _ = q.shape
    t = ATTN_TILE
    hc = ATT_HEAD_COLS
    n_tiles = seq // t
    group = ATTN_GROUP
    steps = n_tiles // group
    span = BLOCK * group // 2
    assert n_tiles % group == 0 and ATT_HEADS * steps * span == seq
    slopes, coef, pos = _alibi_operands()
    per_head_rows = pl.BlockSpec((1, seq, hc), lambda b, h, j: (b, 0, h))
    chunk = lambda rows: pl.BlockSpec((1, rows, span), lambda b, h, j: (b, 0, h * steps + j))
    return pl.pallas_call(
        functools.partial(_mixer_kernel, lam_init=lam_init, n_tiles=n_tiles),
        grid=(nb, ATT_HEADS, steps),
        in_specs=[per_head_rows, per_head_rows,
                  pl.BlockSpec((1, hc, seq), lambda b, h, j: (b, h, 0)),
                  pl.BlockSpec((1, BLOCK, hc), lambda b, h, j: (0, 0, h)),
                  pl.BlockSpec((1, hc, BLOCK), lambda b, h, j: (0, h, 0)),
                  pl.BlockSpec((1, 1, LANES), lambda b, h, j: (h, 0, 0)),
                  pl.BlockSpec((1, LANES, t), lambda b, h, j: (h, 0, 0)),
                  _resident(pos.shape), _resident(lam_vecs.shape), _resident(subln_g.shape),
                  chunk(SSD_CONV_DIM), chunk(SSD_HEADS), chunk(SSD_INNER),
                  _resident(xbct_meta.shape), _resident(dtt_meta.shape)]
                 + [_resident(a.shape) for a in ssd_params],
        out_specs=[per_head_rows,
                   pl.BlockSpec((1, span, SSD_INNER), lambda b, h, j: (b, h * steps + j, 0))],
        out_shape=[jax.ShapeDtypeStruct((nb, seq, ATT_V), BF16),
                   jax.ShapeDtypeStruct((nb, seq, SSD_INNER), BF16)],
        scratch_shapes=[pltpu.VMEM((group, 2, 2 * LANES, t), BF16),
                        pltpu.VMEM((group, 2, 1, t), F32),
                        pltpu.VMEM((group, 2, hc + DENOM_ROWS, t), F32)]
                       + [pltpu.VMEM((2, t, t), F32), pltpu.VMEM((2, 1, t), F32)] * 2
                       + [pltpu.VMEM((SSD_INNER, SSD_STATE), F32),
                          pltpu.VMEM((SSD_INNER, BLOCK), F32)],
        compiler_params=pltpu.CompilerParams(
            dimension_semantics=("parallel", "arbitrary", "arbitrary"),
            vmem_limit_bytes=VMEM_LIMIT_BYTES),
        name="mixers",
    )(q, k, vt, k_meta, vt_meta, slopes, coef, pos, lam_vecs, subln_g, xbct, dtt, zt,
      xbct_meta, dtt_meta, *ssd_params)


def _merge_ffn_kernel(x_ref, ys_ref, ya_ref, gates_ref, gbias_ref, ws_ref, wa_ref, wo_ref,
                      nffn_ref, wg_ref, wu_ref, wd_ref, nfin_ref, o_ref):
    d = x_ref.shape[-1]
    gates = _sigmoid(gates_ref[...].astype(F32) + gbias_ref[...])
    merged = (gates[:, :d] * _dot(ys_ref[...], ws_ref[...])
              + gates[:, d:] * _dot(ya_ref[...], wa_ref[...]))
    h = x_ref[...] + _dot(merged.astype(BF16), wo_ref[...])
    u = h * lax.rsqrt(jnp.mean(h * h, axis=-1, keepdims=True) + NORM_EPS) * nffn_ref[...]
    ub = u.astype(BF16)
    hidden = _silu(_dot(ub, wg_ref[...])) * _dot(ub, wu_ref[...])
    h = h + _dot(hidden.astype(BF16), wd_ref[...])
    o_ref[...] = h * lax.rsqrt(jnp.mean(h * h, axis=-1, keepdims=True) + NORM_EPS) * nfin_ref[...]


def _merge_ffn(x2, ys, ya, gates, consts):
    n, d = x2.shape
    rows = FFN_ROWS
    tile = lambda cols: pl.BlockSpec((rows, cols), lambda i: (i, 0))
    gbias, ws, wa, wo, nffn, wg, wu, wd, nfin = consts
    return pl.pallas_call(
        _merge_ffn_kernel,
        grid=(n // rows,),
        in_specs=[tile(d), tile(ys.shape[1]), tile(ya.shape[1]), tile(gates.shape[1])]
                 + [_resident(a.shape) for a in consts],
        out_specs=tile(d),
        out_shape=jax.ShapeDtypeStruct((n, d), F32),
        compiler_params=pltpu.CompilerParams(
            dimension_semantics=("parallel",), vmem_limit_bytes=VMEM_LIMIT_BYTES),
        name="merge_ffn",
    )(x2, ys, ya, gates, *consts)


def _lane_bcast(v, n=LANES):
    return jnp.broadcast_to(v.astype(F32)[:, None], (v.shape[0], n))


def _layer(x, meta_chunk, l, norm_mix_g, w_in, gate_bias, conv_w, conv_b, dt_bias, a_log, d_skip,
           ssd_norm_g, lam_vecs, subln_g, w_ssd_branch, w_attn_branch, w_out, norm_ffn_g,
           w_gate_ffn, w_up_ffn, w_down_ffn, norm_final_g):
    nb, seq, d = x.shape
    sizes = [SSD_INNER, SSD_CONV_DIM, SSD_HEADS, ATT_QK, ATT_QK, ATT_V, 2 * d]
    offs = [0]
    for s in sizes:
        offs.append(offs[-1] + s)
    col = lambda idx: w_in[:, offs[idx]:offs[idx + 1]].astype(BF16)
    w = {"z_t": col(0).T, "xbc_t": col(1).T, "dt_t": col(2).T, "q": col(3), "k": col(4),
         "v_t": col(5).T, "gates": col(6)}
    g_mix = norm_mix_g.astype(F32)[None, :]

    conv_taps = jnp.stack([_lane_bcast(conv_w[:, t]) for t in range(SSD_CONV)])
    conv_bias = _lane_bcast(conv_b)
    _, k_meta, _, xbct_meta, _, vt_meta, dtt_meta, meta_tail = _in_projection(
        meta_chunk, g_mix, w, conv_taps, conv_bias, jnp.zeros((1, SSD_CONV_DIM, BLOCK), F32),
        rows=BLOCK, masked_rows=PAD_FRONT)
    q, k, gates, xbct, zt, vt, dtt, _ = _in_projection(
        x, g_mix, w, conv_taps, conv_bias, meta_tail, rows=INPROJ_ROWS, masked_rows=0)

    ssd_params = [_lane_bcast(dt_bias), _lane_bcast(a_log),
                  _lane_bcast(jnp.repeat(d_skip, SSD_HEADDIM)), _lane_bcast(ssd_norm_g)]
    lam_init = 0.8 - 0.6 * math.exp(-0.3 * l)
    y_att, y_ssd = _mixers(q, k, vt, k_meta, vt_meta, lam_vecs.astype(F32),
                           _lane_bcast(subln_g, ATTN_TILE), xbct, dtt, zt, xbct_meta, dtt_meta,
                           ssd_params, lam_init=lam_init)

    consts = [gate_bias.astype(F32)[None, :], w_ssd_branch.astype(BF16),
              w_attn_branch.astype(BF16), w_out.astype(BF16), norm_ffn_g.astype(F32)[None, :],
              w_gate_ffn.astype(BF16), w_up_ffn.astype(BF16), w_down_ffn.astype(BF16),
              norm_final_g.astype(F32)[None, :]]
    out = _merge_ffn(x.reshape(nb * seq, d), y_ssd.reshape(nb * seq, SSD_INNER),
                     y_att.reshape(nb * seq, ATT_V), gates.reshape(nb * seq, 2 * d), consts)
    return out.reshape(nb, seq, d)


def kernel(x, meta_tokens, norm_mix_g, w_in, gate_bias, conv_w, conv_b, dt_bias, a_log, d_skip, ssd_norm_g, lambda_q1, lambda_k1, lambda_q2, lambda_k2, subln_g, w_ssd_branch, w_attn_branch, w_out, norm_ffn_g, w_gate_ffn, w_up_ffn, w_down_ffn, norm_final_g):
    depth = w_in.shape[0]
    assert depth == 1, "the fused final norm assumes a single layer"
    d = x.shape[-1]
    meta_chunk = jnp.concatenate(
        [jnp.zeros((PAD_FRONT, d), x.dtype), meta_tokens.astype(x.dtype)], axis=0)[None]
    l = 0
    lam_vecs = jnp.stack([lambda_q1[l], lambda_k1[l], lambda_q2[l], lambda_k2[l]])
    return _layer(x, meta_chunk, l, norm_mix_g[l], w_in[l], gate_bias[l], conv_w[l], conv_b[l],
                  dt_bias[l], a_log[l], d_skip[l], ssd_norm_g[l], lam_vecs, subln_g[l],
                  w_ssd_branch[l], w_attn_branch[l], w_out[l], norm_ffn_g[l], w_gate_ffn[l],
                  w_up_ffn[l], w_down_ffn[l], norm_final_g)
```

```python
import functools
import math

import jax
import jax.numpy as jnp
import numpy as np
from jax import lax
from jax.experimental import pallas as pl
from jax.experimental.pallas import tpu as pltpu

F32 = jnp.float32
BF16 = jnp.bfloat16

N_META = 16
BLOCK = 128
PAD_FRONT = BLOCK - N_META
NORM_EPS = 1e-6
NEG = -1e30

SSD_HEADDIM = 64
SSD_HEADS = 16
SSD_INNER = SSD_HEADS * SSD_HEADDIM
SSD_GROUPS = 2
SSD_STATE = 128
SSD_CONV = 4
SSD_BC = SSD_GROUPS * SSD_STATE
SSD_CONV_DIM = SSD_INNER + 2 * SSD_BC
HEADS_PER_GROUP = SSD_HEADS // SSD_GROUPS
GROUP_INNER = HEADS_PER_GROUP * SSD_HEADDIM

ATT_HEADS = 8
ATT_HEAD_DIM = 64
ATT_QK = ATT_HEADS * 2 * ATT_HEAD_DIM
ATT_V = ATT_HEADS * 2 * ATT_HEAD_DIM
ATT_HEAD_COLS = 2 * ATT_HEAD_DIM

LOG2E = math.log2(math.e)
Q_SCALE = ATT_HEAD_DIM ** -0.5 * LOG2E

LANES = 128
VMEM_LIMIT_BYTES = 56 * 1024 * 1024
INPROJ_ROWS = 512
CONV_ROWS = 64
ATTN_TILE = 512
ATTN_RUN = 8
ATTN_GROUP = 8
POS_RADIX = 16
DENOM_ROWS = 16
FFN_ROWS = 512


def _resident(shape):
    zeros = (0,) * len(shape)
    return pl.BlockSpec(shape, lambda *_: zeros, pipeline_mode=pl.Buffered(1))


def _sigmoid(x):
    return 1.0 / (1.0 + jnp.exp2(x * (-LOG2E)))


def _silu(x):
    return x * _sigmoid(x)


def _dot(a, b):
    return jnp.dot(a, b, preferred_element_type=F32)


def _dot_nt(a, b):
    return lax.dot_general(a, b, (((1,), (1,)), ((), ())), preferred_element_type=F32)


def _inproj_kernel(x_ref, g_ref, wq_ref, wk_ref, wg_ref, wxbct_ref, wzt_ref, wvt_ref, wdtt_ref,
                   convw_ref, convb_ref, hist_ref, q_ref, k_ref, gates_ref, xbct_ref, zt_ref,
                   vt_ref, dtt_ref, tail_ref, raw_ref, *, masked_rows):
    rows = x_ref.shape[1]
    x = x_ref[0]
    ms = jnp.mean(x * x, axis=-1, keepdims=True)
    u = x * lax.rsqrt(ms + NORM_EPS) * g_ref[...]
    if masked_rows:
        row = lax.broadcasted_iota(jnp.int32, u.shape, 0)
        u = jnp.where(row >= masked_rows, u, 0.0)
    ub = u.astype(BF16)

    @pl.when(pl.program_id(1) == 0)
    def _():
        raw_ref[:, 0:BLOCK] = hist_ref[0]

    raw_ref[:, BLOCK:] = _dot_nt(wxbct_ref[...], ub)
    q_ref[0] = (_dot(ub, wq_ref[...]) * Q_SCALE).astype(q_ref.dtype)
    k_ref[0] = _dot(ub, wk_ref[...]).astype(k_ref.dtype)
    gates_ref[0] = _dot(ub, wg_ref[...]).astype(gates_ref.dtype)
    zt_ref[0] = _dot_nt(wzt_ref[...], ub).astype(zt_ref.dtype)
    vt_ref[0] = _dot_nt(wvt_ref[...], ub).astype(vt_ref.dtype)
    dtt_ref[0] = _dot_nt(wdtt_ref[...], ub)
    lane = lax.broadcasted_iota(jnp.int32, (CONV_ROWS, BLOCK), 1)
    for blk in range(SSD_CONV_DIM // CONV_ROWS):
        rs = slice(blk * CONV_ROWS, (blk + 1) * CONV_ROWS)
        taps = [convw_ref[t, rs, :] for t in range(SSD_CONV)]
        bias = convb_ref[rs, :]
        for cb in range(rows // BLOCK):
            x_prev = raw_ref[rs, cb * BLOCK:(cb + 1) * BLOCK]
            xc = raw_ref[rs, (cb + 1) * BLOCK:(cb + 2) * BLOCK]
            acc = bias + taps[SSD_CONV - 1] * xc
            for j in range(1, SSD_CONV):
                joined = jnp.where(lane >= BLOCK - j, x_prev, xc)
                acc = acc + taps[SSD_CONV - 1 - j] * pltpu.roll(joined, j, 1)
            xbct_ref[0, rs, cb * BLOCK:(cb + 1) * BLOCK] = _silu(acc)
        tail = raw_ref[rs, rows:rows + BLOCK]
        raw_ref[rs, 0:BLOCK] = tail
        tail_ref[0, rs, :] = tail


def _in_projection(h, g, w, conv_w, conv_b, hist, *, rows, masked_rows):
    nb, seq, d = h.shape
    grid = (nb, seq // rows)
    row_major = lambda cols, dt: (jax.ShapeDtypeStruct((nb, seq, cols), dt),
                                  pl.BlockSpec((1, rows, cols), lambda b, i: (b, i, 0)))
    chan_major = lambda cols, dt: (jax.ShapeDtypeStruct((nb, cols, seq), dt),
                                   pl.BlockSpec((1, cols, rows), lambda b, i: (b, 0, i)))
    tail = (jax.ShapeDtypeStruct((nb, SSD_CONV_DIM, BLOCK), F32),
            pl.BlockSpec((1, SSD_CONV_DIM, BLOCK), lambda b, i: (b, 0, 0)))
    outs = [row_major(ATT_QK, BF16), row_major(ATT_QK, BF16), row_major(2 * d, BF16),
            chan_major(SSD_CONV_DIM, F32), chan_major(SSD_INNER, BF16), chan_major(ATT_V, BF16),
            chan_major(SSD_HEADS, F32), tail]
    consts = [g, w["q"], w["k"], w["gates"], w["xbc_t"], w["z_t"], w["v_t"], w["dt_t"],
              conv_w, conv_b, hist]
    return pl.pallas_call(
        functools.partial(_inproj_kernel, masked_rows=masked_rows),
        grid=grid,
        in_specs=[pl.BlockSpec((1, rows, d), lambda b, i: (b, i, 0))]
                 + [_resident(a.shape) for a in consts],
        out_specs=[o[1] for o in outs],
        out_shape=[o[0] for o in outs],
        scratch_shapes=[pltpu.VMEM((SSD_CONV_DIM, BLOCK + rows), F32)],
        compiler_params=pltpu.CompilerParams(
            dimension_semantics=("parallel", "arbitrary"), vmem_limit_bytes=VMEM_LIMIT_BYTES),
        name="in_projection",
    )(h, *consts)


def _ssd_scan(xbc_ref, dtt_ref, ts, zt_ref, y_ref, dtb_ref, alog_ref, dskip_ref, ng_ref,
              state_ref, ygate_ref, *, first):
    v = dtt_ref[0, :, ts] + dtb_ref[...]
    dt = jnp.maximum(v, 0.0) + jnp.log1p(jnp.exp(-jnp.abs(v)))
    if first:
        step = lax.broadcasted_iota(jnp.int32, dt.shape, 1)
        dt = jnp.where(step >= PAD_FRONT, dt, 0.0)
    da = dt * (-jnp.exp(alog_ref[...]))
    r = lax.broadcasted_iota(jnp.int32, (BLOCK, BLOCK), 0)
    c = lax.broadcasted_iota(jnp.int32, (BLOCK, BLOCK), 1)
    upper = r <= c
    acs = jnp.dot(da, upper.astype(F32), preferred_element_type=F32,
                  precision=lax.Precision.HIGHEST)
    last = acs[:, BLOCK - 1:BLOCK]
    to_end = jnp.exp(last - acs) * dt
    chunk_decay = jnp.broadcast_to(jnp.exp(last), acs.shape)
    if not first:
        acs_col = jnp.concatenate([acs, jnp.zeros((BLOCK - SSD_HEADS, BLOCK), F32)], axis=0).T
        grow = jnp.exp(acs)
        ssq = jnp.zeros((1, BLOCK), F32)

    for g in range(SSD_GROUPS):
        b_rows = slice(SSD_INNER + g * SSD_STATE, SSD_INNER + (g + 1) * SSD_STATE)
        c_rows = slice(SSD_INNER + SSD_BC + g * SSD_STATE, SSD_INNER + SSD_BC + (g + 1) * SSD_STATE)
        bg_t = xbc_ref[0, b_rows, ts].T.astype(BF16)
        heads = range(g * HEADS_PER_GROUP, (g + 1) * HEADS_PER_GROUP)
        grp = slice(g * GROUP_INNER, (g + 1) * GROUP_INNER)
        per_row = lambda a: jnp.concatenate(
            [jnp.broadcast_to(a[h:h + 1, :], (SSD_HEADDIM, BLOCK)) for h in heads], axis=0)
        inject = _dot((xbc_ref[0, grp, ts] * per_row(to_end)).astype(BF16), bg_t)
        if first:
            state_ref[grp, :] = inject
            continue
        cg = xbc_ref[0, c_rows, ts].astype(BF16)
        cb_t = _dot(bg_t, cg)
        st = state_ref[grp, :]
        state_ref[grp, :] = st * per_row(chunk_decay) + inject
        y_off = _dot(st.astype(BF16), cg)
        for e, h in enumerate(heads):
            rows = slice(h * SSD_HEADDIM, (h + 1) * SSD_HEADDIM)
            xs = xbc_ref[0, rows, ts]
            seg = acs[h:h + 1, :] - acs_col[:, h:h + 1]
            g_t = (cb_t * jnp.where(upper, jnp.exp(seg), 0.0)).astype(BF16)
            y = _dot((xs * dt[h:h + 1, :]).astype(BF16), g_t)
            y = (y + y_off[e * SSD_HEADDIM:(e + 1) * SSD_HEADDIM] * grow[h:h + 1, :]
                 + xs * dskip_ref[rows, :])
            y = y * _silu(zt_ref[0, rows, ts].astype(F32))
            ssq = ssq + jnp.sum(y * y, axis=0, keepdims=True)
            ygate_ref[rows, :] = y

    if not first:
        scale = lax.rsqrt(ssq * (1.0 / SSD_INNER) + NORM_EPS)
        for blk in range(SSD_INNER // BLOCK):
            rs = slice(blk * BLOCK, (blk + 1) * BLOCK)
            y = ygate_ref[rs, :] * scale * ng_ref[rs, :]
            y_ref[0, ts, rs] = y.T.astype(y_ref.dtype)


def _mixer_kernel(q_ref, k_ref, vt_ref, k_meta_ref, vt_meta_ref, slope_ref, coef_ref, pos_ref,
                  lam_ref, g_ref, xbct_ref, dtt_ref, zt_ref, xbct_meta_ref, dtt_meta_ref,
                  dtb_ref, alog_ref, dskip_ref, ng_ref, o_ref, y_ref, qa_ref, m_ref, acc_ref,
                  sa_ref, ca_ref, sb_ref, cb_ref, state_ref, ygate_ref, *, lam_init, n_tiles):
    j = pl.program_id(2)
    scan = functools.partial(_ssd_scan, zt_ref=zt_ref, y_ref=y_ref, dtb_ref=dtb_ref,
                             alog_ref=alog_ref, dskip_ref=dskip_ref, ng_ref=ng_ref,
                             state_ref=state_ref, ygate_ref=ygate_ref)

    @pl.when((pl.program_id(1) == 0) & (j == 0))
    def _():
        scan(xbct_meta_ref, dtt_meta_ref, slice(0, BLOCK), first=True)

    t = ATTN_TILE
    hc = ATT_HEAD_COLS
    group = ATTN_GROUP
    steps = n_tiles // group
    q_tiles = []
    for pair in range(group // 2):
        q_tiles += [j + pair * steps, n_tiles - 1 - j - pair * steps]
    slope = slope_ref[0][:, 0:1]
    key_row = lax.broadcasted_iota(jnp.int32, (t, t), 0)
    qry_col = lax.broadcasted_iota(jnp.int32, (t, t), 1)
    meta_row = lax.broadcasted_iota(jnp.int32, (BLOCK, t), 0)

    def rows(tile):
        return pl.ds(pl.multiple_of(tile * t, t), t)

    coef = coef_ref[0]
    chan = lax.broadcasted_iota(jnp.int32, (hc, t), 0)
    for w, tile in enumerate(q_tiles):
        q_t = q_ref[0, rows(tile), :].astype(F32).T
        for comp, keep in enumerate((chan < ATT_HEAD_DIM, chan >= ATT_HEAD_DIM)):
            qa_ref[w, comp] = jnp.concatenate(
                [jnp.where(keep, q_t, 0.0).astype(BF16), coef], axis=0)
    m_ref[...] = jnp.full(m_ref.shape, NEG, F32)
    acc_ref[...] = jnp.zeros(acc_ref.shape, F32)

    def scores(kt, w, mask):
        k_aug = jnp.concatenate([kt, pos_ref[0:kt.shape[0], :]], axis=1)
        out = []
        for comp in range(2):
            s = _dot(k_aug, qa_ref[w, comp])
            out.append(s if mask is None else jnp.where(mask, s, NEG))
        return out

    def fold(w, s_of, cmax_of, vtt, key_start):
        sigma = slope * lax.convert_element_type(key_start, F32)
        v_aug = jnp.concatenate([vtt, jnp.ones((DENOM_ROWS, vtt.shape[1]), BF16)], axis=0)
        for comp in range(2):
            m_old = m_ref[w, comp]
            m_new = jnp.maximum(m_old, cmax_of(comp) + sigma)
            alpha = jnp.exp2(m_old - m_new)
            pr = jnp.exp2(s_of(comp) - (m_new - sigma))
            acc_ref[w, comp] = alpha * acc_ref[w, comp] + _dot(v_aug, pr.astype(BF16))
            m_ref[w, comp] = m_new

    n_full = (group // 2) * (n_tiles - 1)
    n_items = group + n_full
    lead = (n_items - 1) % ATTN_RUN
    assert ATTN_RUN % 2 == 0 and group - 1 <= ATTN_RUN and n_items - 1 >= ATTN_RUN + lead
    starts = [sum(q_tiles[:w]) for w in range(group)]

    def pick(w, values):
        out = values[0]
        for v in range(1, group):
            out = jnp.where(w == v, values[v], out)
        return out

    def item(r):
        r = jnp.asarray(r, jnp.int32)
        w_full = sum((r - 1 >= starts[v]).astype(jnp.int32) for v in range(1, group))
        w = jnp.where(r == 0, 0, jnp.where(r > n_full, r - n_full, w_full))
        on_diagonal = (r == 0) | (r > n_full)
        key_tile = jnp.where(on_diagonal, pick(w, q_tiles), r - 1 - pick(w_full, starts))
        return w, key_tile, pick(w, q_tiles)

    slots = ((sa_ref, ca_ref), (sb_ref, cb_ref))

    def stage(r, parity, masked):
        w, key_tile, _ = item(r)
        stage_ref, cmax_ref = slots[parity]
        mask = (key_row <= qry_col) if masked else None
        for comp, s in enumerate(scores(k_ref[0, rows(key_tile), :], w, mask)):
            stage_ref[comp] = s
            cmax_ref[comp] = jnp.max(s, axis=0, keepdims=True)

    def fold_staged(r, parity):
        w, key_tile, q_tile = item(r)
        stage_ref, cmax_ref = slots[parity]
        fold(w, lambda comp: stage_ref[comp], lambda comp: cmax_ref[comp],
             vt_ref[0, :, rows(key_tile)], (key_tile - q_tile) * t)

    s_meta = [scores(k_meta_ref[0], w, meta_row >= PAD_FRONT) for w in range(group)]
    stage(0, 0, True)
    for w, tile in enumerate(q_tiles):
        fold(w, lambda comp: s_meta[w][comp],
             lambda comp: jnp.max(s_meta[w][comp], axis=0, keepdims=True),
             vt_meta_ref[0], -(BLOCK + tile * t))

    def steps_from(base, base_parity, count):
        for c in range(1, count + 1):
            stage(base + c, (base_parity + c) % 2, isinstance(base, int) and base + c > n_full)
            fold_staged(base + c - 1, (base_parity + c - 1) % 2)

    def run_body(u, carry):
        steps_from(lead + u * ATTN_RUN, lead % 2, ATTN_RUN)
        return carry

    if lead:
        steps_from(0, 0, lead)
    n_runs = (n_items - 1 - lead) // ATTN_RUN
    lax.fori_loop(0, n_runs - 1, run_body, 0)
    for sub in range(group // 2):
        scan(xbct_ref, dtt_ref, slice(sub * BLOCK, (sub + 1) * BLOCK), first=False)
    steps_from(lead + (n_runs - 1) * ATTN_RUN, lead % 2, ATTN_RUN)
    fold_staged(n_items - 1, (n_items - 1) % 2)

    lam = (jnp.exp(jnp.sum(lam_ref[0:1] * lam_ref[1:2], axis=-1, keepdims=True))
           - jnp.exp(jnp.sum(lam_ref[2:3] * lam_ref[3:4], axis=-1, keepdims=True)) + lam_init)
    for w, tile in enumerate(q_tiles):
        out = [acc_ref[w, comp, 0:hc, :] * (1.0 / acc_ref[w, comp, hc:hc + 1, :])
               for comp in range(2)]
        o = out[0] - lam * out[1]
        ms = jnp.mean(o * o, axis=0, keepdims=True)
        y = o * lax.rsqrt(ms + NORM_EPS) * g_ref[...] * (1.0 - lam_init)
        o_ref[0, rows(tile), :] = y.T.astype(o_ref.dtype)


def _alibi_operands():
    heads = np.arange(ATT_HEADS, dtype=np.float32)
    slope = (2.0 ** (-8.0 * (heads + 1.0) / ATT_HEADS)).astype(np.float32) * np.float32(LOG2E)
    pieces, rest = [], slope
    for _ in range(3):
        piece = rest.astype(BF16).astype(np.float32)
        pieces.append(piece)
        rest = rest - piece
    cols = [np.float32(POS_RADIX) * p for p in pieces] + pieces
    coef = np.zeros((ATT_HEADS, LANES, ATTN_TILE), np.float32)
    coef[:, :len(cols), :] = np.stack(cols, axis=1)[:, :, None]
    row = np.arange(ATTN_TILE)
    pos = np.zeros((ATTN_TILE, LANES), np.float32)
    pos[:, 0:3] = (row // POS_RADIX)[:, None]
    pos[:, 3:6] = (row % POS_RADIX)[:, None]
    slope = np.broadcast_to(slope[:, None, None], (ATT_HEADS, 1, LANES))
    return jnp.asarray(slope), jnp.asarray(coef, dtype=BF16), jnp.asarray(pos, dtype=BF16)


def _mixers(q, k, vt, k_meta, vt_meta, lam_vecs, subln_g, xbct, dtt, zt, xbct_meta, dtt_meta,
            ssd_params, *, lam_init):
    nb, seq, _ = q.shape
    t = ATTN_TILE
    hc = ATT_HEAD_COLS
    n_tiles = seq // t
    group = ATTN_GROUP
    steps = n_tiles // group
    span = BLOCK * group // 2
    assert n_tiles % group == 0 and ATT_HEADS * steps * span == seq
    slopes, coef, pos = _alibi_operands()
    per_head_rows = pl.BlockSpec((1, seq, hc), lambda b, h, j: (b, 0, h))
    chunk = lambda rows: pl.BlockSpec((1, rows, span), lambda b, h, j: (b, 0, h * steps + j))
    return pl.pallas_call(
        functools.partial(_mixer_kernel, lam_init=lam_init, n_tiles=n_tiles),
        grid=(nb, ATT_HEADS, steps),
        in_specs=[per_head_rows, per_head_rows,
                  pl.BlockSpec((1, hc, seq), lambda b, h, j: (b, h, 0)),
                  pl.BlockSpec((1, BLOCK, hc), lambda b, h, j: (0, 0, h)),
                  pl.BlockSpec((1, hc, BLOCK), lambda b, h, j: (0, h, 0)),
                  pl.BlockSpec((1, 1, LANES), lambda b, h, j: (h, 0, 0)),
                  pl.BlockSpec((1, LANES, t), lambda b, h, j: (h, 0, 0)),
                  _resident(pos.shape), _resident(lam_vecs.shape), _resident(subln_g.shape),
                  chunk(SSD_CONV_DIM), chunk(SSD_HEADS), chunk(SSD_INNER),
                  _resident(xbct_meta.shape), _resident(dtt_meta.shape)]
                 + [_resident(a.shape) for a in ssd_params],
        out_specs=[per_head_rows,
                   pl.BlockSpec((1, span, SSD_INNER), lambda b, h, j: (b, h * steps + j, 0))],
        out_shape=[jax.ShapeDtypeStruct((nb, seq, ATT_V), BF16),
                   jax.ShapeDtypeStruct((nb, seq, SSD_INNER), BF16)],
        scratch_shapes=[pltpu.VMEM((group, 2, 2 * LANES, t), BF16),
                        pltpu.VMEM((group, 2, 1, t), F32),
                        pltpu.VMEM((group, 2, hc + DENOM_ROWS, t), F32)]
                       + [pltpu.VMEM((2, t, t), F32), pltpu.VMEM((2, 1, t), F32)] * 2
                       + [pltpu.VMEM((SSD_INNER, SSD_STATE), F32),
                          pltpu.VMEM((SSD_INNER, BLOCK), F32)],
        compiler_params=pltpu.CompilerParams(
            dimension_semantics=("parallel", "arbitrary", "arbitrary"),
            vmem_limit_bytes=VMEM_LIMIT_BYTES),
        name="mixers",
    )(q, k, vt, k_meta, vt_meta, slopes, coef, pos, lam_vecs, subln_g, xbct, dtt, zt,
      xbct_meta, dtt_meta, *ssd_params)


def _merge_ffn_kernel(x_ref, ys_ref, ya_ref, gates_ref, gbias_ref, ws_ref, wa_ref, wo_ref,
                      nffn_ref, wg_ref, wu_ref, wd_ref, nfin_ref, o_ref):
    d = x_ref.shape[-1]
    gates = _sigmoid(gates_ref[...].astype(F32) + gbias_ref[...])
    merged = (gates[:, :d] * _dot(ys_ref[...], ws_ref[...])
              + gates[:, d:] * _dot(ya_ref[...], wa_ref[...]))
    h = x_ref[...] + _dot(merged.astype(BF16), wo_ref[...])
    u = h * lax.rsqrt(jnp.mean(h * h, axis=-1, keepdims=True) + NORM_EPS) * nffn_ref[...]
    ub = u.astype(BF16)
    hidden = _silu(_dot(ub, wg_ref[...])) * _dot(ub, wu_ref[...])
    h = h + _dot(hidden.astype(BF16), wd_ref[...])
    o_ref[...] = h * lax.rsqrt(jnp.mean(h * h, axis=-1, keepdims=True) + NORM_EPS) * nfin_ref[...]


def _merge_ffn(x2, ys, ya, gates, consts):
    n, d = x2.shape
    rows = FFN_ROWS
    tile = lambda cols: pl.BlockSpec((rows, cols), lambda i: (i, 0))
    gbias, ws, wa, wo, nffn, wg, wu, wd, nfin = consts
    return pl.pallas_call(
        _merge_ffn_kernel,
        grid=(n // rows,),
        in_specs=[tile(d), tile(ys.shape[1]), tile(ya.shape[1]), tile(gates.shape[1])]
                 + [_resident(a.shape) for a in consts],
        out_specs=tile(d),
        out_shape=jax.ShapeDtypeStruct((n, d), F32),
        compiler_params=pltpu.CompilerParams(
            dimension_semantics=("parallel",), vmem_limit_bytes=VMEM_LIMIT_BYTES),
        name="merge_ffn",
    )(x2, ys, ya, gates, *consts)


def _lane_bcast(v, n=LANES):
    return jnp.broadcast_to(v.astype(F32)[:, None], (v.shape[0], n))


def _layer(x, meta_chunk, l, norm_mix_g, w_in, gate_bias, conv_w, conv_b, dt_bias, a_log, d_skip,
           ssd_norm_g, lam_vecs, subln_g, w_ssd_branch, w_attn_branch, w_out, norm_ffn_g,
           w_gate_ffn, w_up_ffn, w_down_ffn, norm_final_g):
    nb, seq, d = x.shape
    sizes = [SSD_INNER, SSD_CONV_DIM, SSD_HEADS, ATT_QK, ATT_QK, ATT_V, 2 * d]
    offs = [0]
    for s in sizes:
        offs.append(offs[-1] + s)
    col = lambda idx: w_in[:, offs[idx]:offs[idx + 1]].astype(BF16)
    w = {"z_t": col(0).T, "xbc_t": col(1).T, "dt_t": col(2).T, "q": col(3), "k": col(4),
         "v_t": col(5).T, "gates": col(6)}
    g_mix = norm_mix_g.astype(F32)[None, :]

    conv_taps = jnp.stack([_lane_bcast(conv_w[:, t]) for t in range(SSD_CONV)])
    conv_bias = _lane_bcast(conv_b)
    _, k_meta, _, xbct_meta, _, vt_meta, dtt_meta, meta_tail = _in_projection(
        meta_chunk, g_mix, w, conv_taps, conv_bias, jnp.zeros((1, SSD_CONV_DIM, BLOCK), F32),
        rows=BLOCK, masked_rows=PAD_FRONT)
    q, k, gates, xbct, zt, vt, dtt, _ = _in_projection(
        x, g_mix, w, conv_taps, conv_bias, meta_tail, rows=INPROJ_ROWS, masked_rows=0)

    ssd_params = [_lane_bcast(dt_bias), _lane_bcast(a_log),
                  _lane_bcast(jnp.repeat(d_skip, SSD_HEADDIM)), _lane_bcast(ssd_norm_g)]
    lam_init = 0.8 - 0.6 * math.exp(-0.3 * l)
    y_att, y_ssd = _mixers(q, k, vt, k_meta, vt_meta, lam_vecs.astype(F32),
                           _lane_bcast(subln_g, ATTN_TILE), xbct, dtt, zt, xbct_meta, dtt_meta,
                           ssd_params, lam_init=lam_init)

    consts = [gate_bias.astype(F32)[None, :], w_ssd_branch.astype(BF16),
              w_attn_branch.astype(BF16), w_out.astype(BF16), norm_ffn_g.astype(F32)[None, :],
              w_gate_ffn.astype(BF16), w_up_ffn.astype(BF16), w_down_ffn.astype(BF16),
              norm_final_g.astype(F32)[None, :]]
    out = _merge_ffn(x.reshape(nb * seq, d), y_ssd.reshape(nb * seq, SSD_INNER),
                     y_att.reshape(nb * seq, ATT_V), gates.reshape(nb * seq, 2 * d), consts)
    return out.reshape(nb, seq, d)


def kernel(x, meta_tokens, norm_mix_g, w_in, gate_bias, conv_w, conv_b, dt_bias, a_log, d_skip, ssd_norm_g, lambda_q1, lambda_k1, lambda_q2, lambda_k2, subln_g, w_ssd_branch, w_attn_branch, w_out, norm_ffn_g, w_gate_ffn, w_up_ffn, w_down_ffn, norm_final_g):
    depth = w_in.shape[0]
    assert depth == 1, "the fused final norm assumes a single layer"
    d = x.shape[-1]
    meta_chunk = jnp.concatenate(
        [jnp.zeros((PAD_FRONT, d), x.dtype), meta_tokens.astype(x.dtype)], axis=0)[None]
    l = 0
    lam_vecs = jnp.stack([lambda_q1[l], lambda_k1[l], lambda_q2[l], lambda_k2[l]])
    return _layer(x, meta_chunk, l, norm_mix_g[l], w_in[l], gate_bias[l], conv_w[l], conv_b[l],
                  dt_bias[l], a_log[l], d_skip[l], ssd_norm_g[l], lam_vecs, subln_g[l],
                  w_ssd_branch[l], w_attn_branch[l], w_out[l], norm_ffn_g[l], w_gate_ffn[l],
                  w_up_ffn[l], w_down_ffn[l], norm_final_g)
```

```python
import functools
import math

import jax
import jax.numpy as jnp
import numpy as np
from jax import lax
from jax.experimental import pallas as pl
from jax.experimental.pallas import tpu as pltpu

F32 = jnp.float32
BF16 = jnp.bfloat16

N_META = 16
BLOCK = 128
PAD_FRONT = BLOCK - N_META
NORM_EPS = 1e-6
NEG = -1e30

SSD_HEADDIM = 64
SSD_HEADS = 16
SSD_INNER = SSD_HEADS * SSD_HEADDIM
SSD_GROUPS = 2
SSD_STATE = 128
SSD_CONV = 4
SSD_BC = SSD_GROUPS * SSD_STATE
SSD_CONV_DIM = SSD_INNER + 2 * SSD_BC
HEADS_PER_GROUP = SSD_HEADS // SSD_GROUPS
GROUP_INNER = HEADS_PER_GROUP * SSD_HEADDIM

ATT_HEADS = 8
ATT_HEAD_DIM = 64
ATT_QK = ATT_HEADS * 2 * ATT_HEAD_DIM
ATT_V = ATT_HEADS * 2 * ATT_HEAD_DIM
ATT_HEAD_COLS = 2 * ATT_HEAD_DIM

LOG2E = math.log2(math.e)
Q_SCALE = ATT_HEAD_DIM ** -0.5 * LOG2E

LANES = 128
VMEM_LIMIT_BYTES = 56 * 1024 * 1024
INPROJ_ROWS = 512
CONV_ROWS = 64
ATTN_TILE = 512
ATTN_RUN = 8
ATTN_GROUP = 8
POS_RADIX = 16
DENOM_ROWS = 16
FFN_ROWS = 512


def _resident(shape):
    zeros = (0,) * len(shape)
    return pl.BlockSpec(shape, lambda *_: zeros, pipeline_mode=pl.Buffered(1))


def _sigmoid(x):
    return 1.0 / (1.0 + jnp.exp2(x * (-LOG2E)))


def _silu(x):
    return x * _sigmoid(x)


def _dot(a, b):
    return jnp.dot(a, b, preferred_element_type=F32)


def _dot_nt(a, b):
    return lax.dot_general(a, b, (((1,), (1,)), ((), ())), preferred_element_type=F32)


def _inproj_kernel(x_ref, g_ref, wq_ref, wk_ref, wg_ref, wxbct_ref, wzt_ref, wvt_ref, wdtt_ref,
                   convw_ref, convb_ref, hist_ref, q_ref, k_ref, gates_ref, xbct_ref, zt_ref,
                   vt_ref, dtt_ref, tail_ref, raw_ref, *, masked_rows):
    rows = x_ref.shape[1]
    x = x_ref[0]
    ms = jnp.mean(x * x, axis=-1, keepdims=True)
    u = x * lax.rsqrt(ms + NORM_EPS) * g_ref[...]
    if masked_rows:
        row = lax.broadcasted_iota(jnp.int32, u.shape, 0)
        u = jnp.where(row >= masked_rows, u, 0.0)
    ub = u.astype(BF16)

    @pl.when(pl.program_id(1) == 0)
    def _():
        raw_ref[:, 0:BLOCK] = hist_ref[0]

    raw_ref[:, BLOCK:] = _dot_nt(wxbct_ref[...], ub)
    q_ref[0] = (_dot(ub, wq_ref[...]) * Q_SCALE).astype(q_ref.dtype)
    k_ref[0] = _dot(ub, wk_ref[...]).astype(k_ref.dtype)
    gates_ref[0] = _dot(ub, wg_ref[...]).astype(gates_ref.dtype)
    zt_ref[0] = _dot_nt(wzt_ref[...], ub).astype(zt_ref.dtype)
    vt_ref[0] = _dot_nt(wvt_ref[...], ub).astype(vt_ref.dtype)
    dtt_ref[0] = _dot_nt(wdtt_ref[...], ub)
    lane = lax.broadcasted_iota(jnp.int32, (CONV_ROWS, BLOCK), 1)
    for blk in range(SSD_CONV_DIM // CONV_ROWS):
        rs = slice(blk * CONV_ROWS, (blk + 1) * CONV_ROWS)
        taps = [convw_ref[t, rs, :] for t in range(SSD_CONV)]
        bias = convb_ref[rs, :]
        for cb in range(rows // BLOCK):
            x_prev = raw_ref[rs, cb * BLOCK:(cb + 1) * BLOCK]
            xc = raw_ref[rs, (cb + 1) * BLOCK:(cb + 2) * BLOCK]
            acc = bias + taps[SSD_CONV - 1] * xc
            for j in range(1, SSD_CONV):
                joined = jnp.where(lane >= BLOCK - j, x_prev, xc)
                acc = acc + taps[SSD_CONV - 1 - j] * pltpu.roll(joined, j, 1)
            xbct_ref[0, rs, cb * BLOCK:(cb + 1) * BLOCK] = _silu(acc)
        tail = raw_ref[rs, rows:rows + BLOCK]
        raw_ref[rs, 0:BLOCK] = tail
        tail_ref[0, rs, :] = tail


def _in_projection(h, g, w, conv_w, conv_b, hist, *, rows, masked_rows):
    nb, seq, d = h.shape
    grid = (nb, seq // rows)
    row_major = lambda cols, dt: (jax.ShapeDtypeStruct((nb, seq, cols), dt),
                                  pl.BlockSpec((1, rows, cols), lambda b, i: (b, i, 0)))
    chan_major = lambda cols, dt: (jax.ShapeDtypeStruct((nb, cols, seq), dt),
                                   pl.BlockSpec((1, cols, rows), lambda b, i: (b, 0, i)))
    tail = (jax.ShapeDtypeStruct((nb, SSD_CONV_DIM, BLOCK), F32),
            pl.BlockSpec((1, SSD_CONV_DIM, BLOCK), lambda b, i: (b, 0, 0)))
    outs = [row_major(ATT_QK, BF16), row_major(ATT_QK, BF16), row_major(2 * d, BF16),
            chan_major(SSD_CONV_DIM, F32), chan_major(SSD_INNER, BF16), chan_major(ATT_V, BF16),
            chan_major(SSD_HEADS, F32), tail]
    consts = [g, w["q"], w["k"], w["gates"], w["xbc_t"], w["z_t"], w["v_t"], w["dt_t"],
              conv_w, conv_b, hist]
    return pl.pallas_call(
        functools.partial(_inproj_kernel, masked_rows=masked_rows),
        grid=grid,
        in_specs=[pl.BlockSpec((1, rows, d), lambda b, i: (b, i, 0))]
                 + [_resident(a.shape) for a in consts],
        out_specs=[o[1] for o in outs],
        out_shape=[o[0] for o in outs],
        scratch_shapes=[pltpu.VMEM((SSD_CONV_DIM, BLOCK + rows), F32)],
        compiler_params=pltpu.CompilerParams(
            dimension_semantics=("parallel", "arbitrary"), vmem_limit_bytes=VMEM_LIMIT_BYTES),
        name="in_projection",
    )(h, *consts)


def _ssd_scan(xbc_ref, dtt_ref, ts, zt_ref, y_ref, dtb_ref, alog_ref, dskip_ref, ng_ref,
              state_ref, ygate_ref, *, first):
    v = dtt_ref[0, :, ts] + dtb_ref[...]
    dt = jnp.maximum(v, 0.0) + jnp.log1p(jnp.exp(-jnp.abs(v)))
    if first:
        step = lax.broadcasted_iota(jnp.int32, dt.shape, 1)
        dt = jnp.where(step >= PAD_FRONT, dt, 0.0)
    da = dt * (-jnp.exp(alog_ref[...]))
    r = lax.broadcasted_iota(jnp.int32, (BLOCK, BLOCK), 0)
    c = lax.broadcasted_iota(jnp.int32, (BLOCK, BLOCK), 1)
    upper = r <= c
    acs = jnp.dot(da, upper.astype(F32), preferred_element_type=F32,
                  precision=lax.Precision.HIGHEST)
    last = acs[:, BLOCK - 1:BLOCK]
    to_end = jnp.exp(last - acs) * dt
    chunk_decay = jnp.broadcast_to(jnp.exp(last), acs.shape)
    if not first:
        acs_col = jnp.concatenate([acs, jnp.zeros((BLOCK - SSD_HEADS, BLOCK), F32)], axis=0).T
        grow = jnp.exp(acs)
        ssq = jnp.zeros((1, BLOCK), F32)

    for g in range(SSD_GROUPS):
        b_rows = slice(SSD_INNER + g * SSD_STATE, SSD_INNER + (g + 1) * SSD_STATE)
        c_rows = slice(SSD_INNER + SSD_BC + g * SSD_STATE, SSD_INNER + SSD_BC + (g + 1) * SSD_STATE)
        bg_t = xbc_ref[0, b_rows, ts].T.astype(BF16)
        heads = range(g * HEADS_PER_GROUP, (g + 1) * HEADS_PER_GROUP)
        grp = slice(g * GROUP_INNER, (g + 1) * GROUP_INNER)
        per_row = lambda a: jnp.concatenate(
            [jnp.broadcast_to(a[h:h + 1, :], (SSD_HEADDIM, BLOCK)) for h in heads], axis=0)
        inject = _dot((xbc_ref[0, grp, ts] * per_row(to_end)).astype(BF16), bg_t)
        if first:
            state_ref[grp, :] = inject
            continue
        cg = xbc_ref[0, c_rows, ts].astype(BF16)
        cb_t = _dot(bg_t, cg)
        st = state_ref[grp, :]
        state_ref[grp, :] = st * per_row(chunk_decay) + inject
        y_off = _dot(st.astype(BF16), cg)
        for e, h in enumerate(heads):
            rows = slice(h * SSD_HEADDIM, (h + 1) * SSD_HEADDIM)
            xs = xbc_ref[0, rows, ts]
            seg = acs[h:h + 1, :] - acs_col[:, h:h + 1]
            g_t = (cb_t * jnp.where(upper, jnp.exp(seg), 0.0)).astype(BF16)
            y = _dot((xs * dt[h:h + 1, :]).astype(BF16), g_t)
            y = (y + y_off[e * SSD_HEADDIM:(e + 1) * SSD_HEADDIM] * grow[h:h + 1, :]
                 + xs * dskip_ref[rows, :])
            y = y * _silu(zt_ref[0, rows, ts].astype(F32))
            ssq = ssq + jnp.sum(y * y, axis=0, keepdims=True)
            ygate_ref[rows, :] = y

    if not first:
        scale = lax.rsqrt(ssq * (1.0 / SSD_INNER) + NORM_EPS)
        for blk in range(SSD_INNER // BLOCK):
            rs = slice(blk * BLOCK, (blk + 1) * BLOCK)
            y = ygate_ref[rs, :] * scale * ng_ref[rs, :]
            y_ref[0, ts, rs] = y.T.astype(y_ref.dtype)


def _mixer_kernel(q_ref, k_ref, vt_ref, k_meta_ref, vt_meta_ref, slope_ref, coef_ref, pos_ref,
                  lam_ref, g_ref, xbct_ref, dtt_ref, zt_ref, xbct_meta_ref, dtt_meta_ref,
                  dtb_ref, alog_ref, dskip_ref, ng_ref, o_ref, y_ref, qa_ref, m_ref, acc_ref,
                  sa_ref, ca_ref, sb_ref, cb_ref, state_ref, ygate_ref, *, lam_init, n_tiles):
    j = pl.program_id(2)
    scan = functools.partial(_ssd_scan, zt_ref=zt_ref, y_ref=y_ref, dtb_ref=dtb_ref,
                             alog_ref=alog_ref, dskip_ref=dskip_ref, ng_ref=ng_ref,
                             state_ref=state_ref, ygate_ref=ygate_ref)

    @pl.when((pl.program_id(1) == 0) & (j == 0))
    def _():
        scan(xbct_meta_ref, dtt_meta_ref, slice(0, BLOCK), first=True)

    t = ATTN_TILE
    hc = ATT_HEAD_COLS
    group = ATTN_GROUP
    steps = n_tiles // group
    q_tiles = []
    for pair in range(group // 2):
        q_tiles += [j + pair * steps, n_tiles - 1 - j - pair * steps]
    slope = slope_ref[0][:, 0:1]
    key_row = lax.broadcasted_iota(jnp.int32, (t, t), 0)
    qry_col = lax.broadcasted_iota(jnp.int32, (t, t), 1)
    meta_row = lax.broadcasted_iota(jnp.int32, (BLOCK, t), 0)

    def rows(tile):
        return pl.ds(pl.multiple_of(tile * t, t), t)

    coef = coef_ref[0]
    chan = lax.broadcasted_iota(jnp.int32, (hc, t), 0)
    for w, tile in enumerate(q_tiles):
        q_t = q_ref[0, rows(tile), :].astype(F32).T
        for comp, keep in enumerate((chan < ATT_HEAD_DIM, chan >= ATT_HEAD_DIM)):
            qa_ref[w, comp] = jnp.concatenate(
                [jnp.where(keep, q_t, 0.0).astype(BF16), coef], axis=0)
    m_ref[...] = jnp.full(m_ref.shape, NEG, F32)
    acc_ref[...] = jnp.zeros(acc_ref.shape, F32)

    half = t // 2

    def scores(kt, w, mask, diagonal=False):
        k_aug = jnp.concatenate([kt, pos_ref[0:kt.shape[0], :]], axis=1)
        out = []
        for comp in range(2):
            if diagonal:
                late = jnp.concatenate([jnp.zeros((half, half), F32),
                                        _dot(k_aug[half:], qa_ref[w, comp, :, half:])], axis=1)
                s = jnp.concatenate([_dot(k_aug[:half], qa_ref[w, comp]), late], axis=0)
            else:
                s = _dot(k_aug, qa_ref[w, comp])
            out.append(s if mask is None else jnp.where(mask, s, NEG))
        return out

    def fold(w, s_of, cmax_of, vtt, key_start, diagonal=False):
        sigma = slope * lax.convert_element_type(key_start, F32)
        v_aug = jnp.concatenate([vtt, jnp.ones((DENOM_ROWS, vtt.shape[1]), BF16)], axis=0)
        for comp in range(2):
            m_old = m_ref[w, comp]
            m_new = jnp.maximum(m_old, cmax_of(comp) + sigma)
            alpha = jnp.exp2(m_old - m_new)
            pb = jnp.exp2(s_of(comp) - (m_new - sigma)).astype(BF16)
            if diagonal:
                update = jnp.concatenate([_dot(v_aug[:, :half], pb[:half, :half]),
                                          _dot(v_aug, pb[:, half:])], axis=1)
            else:
                update = _dot(v_aug, pb)
            acc_ref[w, comp] = alpha * acc_ref[w, comp] + update
            m_ref[w, comp] = m_new

    n_full = (group // 2) * (n_tiles - 1)
    n_items = group + n_full
    lead = (n_items - 1) % ATTN_RUN
    assert ATTN_RUN % 2 == 0 and group - 1 <= ATTN_RUN and n_items - 1 >= ATTN_RUN + lead
    starts = [sum(q_tiles[:w]) for w in range(group)]

    def pick(w, values):
        out = values[0]
        for v in range(1, group):
            out = jnp.where(w == v, values[v], out)
        return out

    def item(r):
        r = jnp.asarray(r, jnp.int32)
        w_full = sum((r - 1 >= starts[v]).astype(jnp.int32) for v in range(1, group))
        w = jnp.where(r == 0, 0, jnp.where(r > n_full, r - n_full, w_full))
        on_diagonal = (r == 0) | (r > n_full)
        key_tile = jnp.where(on_diagonal, pick(w, q_tiles), r - 1 - pick(w_full, starts))
        return w, key_tile, pick(w, q_tiles)

    slots = ((sa_ref, ca_ref), (sb_ref, cb_ref))

    def stage(r, parity, masked):
        w, key_tile, _ = item(r)
        stage_ref, cmax_ref = slots[parity]
        mask = (key_row <= qry_col) if masked else None
        for comp, s in enumerate(scores(k_ref[0, rows(key_tile), :], w, mask, masked)):
            stage_ref[comp] = s
            cmax_ref[comp] = jnp.max(s, axis=0, keepdims=True)

    def fold_staged(r, parity):
        w, key_tile, q_tile = item(r)
        stage_ref, cmax_ref = slots[parity]
        fold(w, lambda comp: stage_ref[comp], lambda comp: cmax_ref[comp],
             vt_ref[0, :, rows(key_tile)], (key_tile - q_tile) * t,
             diagonal=isinstance(r, int) and (r == 0 or r > n_full))

    s_meta = [scores(k_meta_ref[0], w, meta_row >= PAD_FRONT) for w in range(group)]
    stage(0, 0, True)
    for w, tile in enumerate(q_tiles):
        fold(w, lambda comp: s_meta[w][comp],
             lambda comp: jnp.max(s_meta[w][comp], axis=0, keepdims=True),
             vt_meta_ref[0], -(BLOCK + tile * t))

    def steps_from(base, base_parity, count):
        for c in range(1, count + 1):
            stage(base + c, (base_parity + c) % 2, isinstance(base, int) and base + c > n_full)
            fold_staged(base + c - 1, (base_parity + c - 1) % 2)

    def run_body(u, carry):
        steps_from(lead + u * ATTN_RUN, lead % 2, ATTN_RUN)
        return carry

    if lead:
        steps_from(0, 0, lead)
    n_runs = (n_items - 1 - lead) // ATTN_RUN
    lax.fori_loop(0, n_runs - 1, run_body, 0)
    for sub in range(group // 2):
        scan(xbct_ref, dtt_ref, slice(sub * BLOCK, (sub + 1) * BLOCK), first=False)
    steps_from(lead + (n_runs - 1) * ATTN_RUN, lead % 2, ATTN_RUN)
    fold_staged(n_items - 1, (n_items - 1) % 2)

    lam = (jnp.exp(jnp.sum(lam_ref[0:1] * lam_ref[1:2], axis=-1, keepdims=True))
           - jnp.exp(jnp.sum(lam_ref[2:3] * lam_ref[3:4], axis=-1, keepdims=True)) + lam_init)
    for w, tile in enumerate(q_tiles):
        out = [acc_ref[w, comp, 0:hc, :] * (1.0 / acc_ref[w, comp, hc:hc + 1, :])
               for comp in range(2)]
        o = out[0] - lam * out[1]
        ms = jnp.mean(o * o, axis=0, keepdims=True)
        y = o * lax.rsqrt(ms + NORM_EPS) * g_ref[...] * (1.0 - lam_init)
        o_ref[0, rows(tile), :] = y.T.astype(o_ref.dtype)


def _alibi_operands():
    heads = np.arange(ATT_HEADS, dtype=np.float32)
    slope = (2.0 ** (-8.0 * (heads + 1.0) / ATT_HEADS)).astype(np.float32) * np.float32(LOG2E)
    pieces, rest = [], slope
    for _ in range(3):
        piece = rest.astype(BF16).astype(np.float32)
        pieces.append(piece)
        rest = rest - piece
    cols = [np.float32(POS_RADIX) * p for p in pieces] + pieces
    coef = np.zeros((ATT_HEADS, LANES, ATTN_TILE), np.float32)
    coef[:, :len(cols), :] = np.stack(cols, axis=1)[:, :, None]
    row = np.arange(ATTN_TILE)
    pos = np.zeros((ATTN_TILE, LANES), np.float32)
    pos[:, 0:3] = (row // POS_RADIX)[:, None]
    pos[:, 3:6] = (row % POS_RADIX)[:, None]
    slope = np.broadcast_to(slope[:, None, None], (ATT_HEADS, 1, LANES))
    return jnp.asarray(slope), jnp.asarray(coef, dtype=BF16), jnp.asarray(pos, dtype=BF16)


def _mixers(q, k, vt, k_meta, vt_meta, lam_vecs, subln_g, xbct, dtt, zt, xbct_meta, dtt_meta,
            ssd_params, *, lam_init):
    nb, seq, _ = q.shape
    t = ATTN_TILE
    hc = ATT_HEAD_COLS
    n_tiles = seq // t
    group = ATTN_GROUP
    steps = n_tiles // group
    span = BLOCK * group // 2
    assert n_tiles % group == 0 and ATT_HEADS * steps * span == seq
    slopes, coef, pos = _alibi_operands()
    per_head_rows = pl.BlockSpec((1, seq, hc), lambda b, h, j: (b, 0, h))
    chunk = lambda rows: pl.BlockSpec((1, rows, span), lambda b, h, j: (b, 0, h * steps + j))
    return pl.pallas_call(
        functools.partial(_mixer_kernel, lam_init=lam_init, n_tiles=n_tiles),
        grid=(nb, ATT_HEADS, steps),
        in_specs=[per_head_rows, per_head_rows,
                  pl.BlockSpec((1, hc, seq), lambda b, h, j: (b, h, 0)),
                  pl.BlockSpec((1, BLOCK, hc), lambda b, h, j: (0, 0, h)),
                  pl.BlockSpec((1, hc, BLOCK), lambda b, h, j: (0, h, 0)),
                  pl.BlockSpec((1, 1, LANES), lambda b, h, j: (h, 0, 0)),
                  pl.BlockSpec((1, LANES, t), lambda b, h, j: (h, 0, 0)),
                  _resident(pos.shape), _resident(lam_vecs.shape), _resident(subln_g.shape),
                  chunk(SSD_CONV_DIM), chunk(SSD_HEADS), chunk(SSD_INNER),
                  _resident(xbct_meta.shape), _resident(dtt_meta.shape)]
                 + [_resident(a.shape) for a in ssd_params],
        out_specs=[per_head_rows,
                   pl.BlockSpec((1, span, SSD_INNER), lambda b, h, j: (b, h * steps + j, 0))],
        out_shape=[jax.ShapeDtypeStruct((nb, seq, ATT_V), BF16),
                   jax.ShapeDtypeStruct((nb, seq, SSD_INNER), BF16)],
        scratch_shapes=[pltpu.VMEM((group, 2, 2 * LANES, t), BF16),
                        pltpu.VMEM((group, 2, 1, t), F32),
                        pltpu.VMEM((group, 2, hc + DENOM_ROWS, t), F32)]
                       + [pltpu.VMEM((2, t, t), F32), pltpu.VMEM((2, 1, t), F32)] * 2
                       + [pltpu.VMEM((SSD_INNER, SSD_STATE), F32),
                          pltpu.VMEM((SSD_INNER, BLOCK), F32)],
        compiler_params=pltpu.CompilerParams(
            dimension_semantics=("parallel", "arbitrary", "arbitrary"),
            vmem_limit_bytes=VMEM_LIMIT_BYTES),
        name="mixers",
    )(q, k, vt, k_meta, vt_meta, slopes, coef, pos, lam_vecs, subln_g, xbct, dtt, zt,
      xbct_meta, dtt_meta, *ssd_params)


def _merge_ffn_kernel(x_ref, ys_ref, ya_ref, gates_ref, gbias_ref, ws_ref, wa_ref, wo_ref,
                      nffn_ref, wg_ref, wu_ref, wd_ref, nfin_ref, o_ref):
    d = x_ref.shape[-1]
    gates = _sigmoid(gates_ref[...].astype(F32) + gbias_ref[...])
    merged = (gates[:, :d] * _dot(ys_ref[...], ws_ref[...])
              + gates[:, d:] * _dot(ya_ref[...], wa_ref[...]))
    h = x_ref[...] + _dot(merged.astype(BF16), wo_ref[...])
    u = h * lax.rsqrt(jnp.mean(h * h, axis=-1, keepdims=True) + NORM_EPS) * nffn_ref[...]
    ub = u.astype(BF16)
    hidden = _silu(_dot(ub, wg_ref[...])) * _dot(ub, wu_ref[...])
    h = h + _dot(hidden.astype(BF16), wd_ref[...])
    o_ref[...] = h * lax.rsqrt(jnp.mean(h * h, axis=-1, keepdims=True) + NORM_EPS) * nfin_ref[...]


def _merge_ffn(x2, ys, ya, gates, consts):
    n, d = x2.shape
    rows = FFN_ROWS
    tile = lambda cols: pl.BlockSpec((rows, cols), lambda i: (i, 0))
    gbias, ws, wa, wo, nffn, wg, wu, wd, nfin = consts
    return pl.pallas_call(
        _merge_ffn_kernel,
        grid=(n // rows,),
        in_specs=[tile(d), tile(ys.shape[1]), tile(ya.shape[1]), tile(gates.shape[1])]
                 + [_resident(a.shape) for a in consts],
        out_specs=tile(d),
        out_shape=jax.ShapeDtypeStruct((n, d), F32),
        compiler_params=pltpu.CompilerParams(
            dimension_semantics=("parallel",), vmem_limit_bytes=VMEM_LIMIT_BYTES),
        name="merge_ffn",
    )(x2, ys, ya, gates, *consts)


def _lane_bcast(v, n=LANES):
    return jnp.broadcast_to(v.astype(F32)[:, None], (v.shape[0], n))


def _layer(x, meta_chunk, l, norm_mix_g, w_in, gate_bias, conv_w, conv_b, dt_bias, a_log, d_skip,
           ssd_norm_g, lam_vecs, subln_g, w_ssd_branch, w_attn_branch, w_out, norm_ffn_g,
           w_gate_ffn, w_up_ffn, w_down_ffn, norm_final_g):
    nb, seq, d = x.shape
    sizes = [SSD_INNER, SSD_CONV_DIM, SSD_HEADS, ATT_QK, ATT_QK, ATT_V, 2 * d]
    offs = [0]
    for s in sizes:
        offs.append(offs[-1] + s)
    col = lambda idx: w_in[:, offs[idx]:offs[idx + 1]].astype(BF16)
    w = {"z_t": col(0).T, "xbc_t": col(1).T, "dt_t": col(2).T, "q": col(3), "k": col(4),
         "v_t": col(5).T, "gates": col(6)}
    g_mix = norm_mix_g.astype(F32)[None, :]

    conv_taps = jnp.stack([_lane_bcast(conv_w[:, t]) for t in range(SSD_CONV)])
    conv_bias = _lane_bcast(conv_b)
    _, k_meta, _, xbct_meta, _, vt_meta, dtt_meta, meta_tail = _in_projection(
        meta_chunk, g_mix, w, conv_taps, conv_bias, jnp.zeros((1, SSD_CONV_DIM, BLOCK), F32),
        rows=BLOCK, masked_rows=PAD_FRONT)
    q, k, gates, xbct, zt, vt, dtt, _ = _in_projection(
        x, g_mix, w, conv_taps, conv_bias, meta_tail, rows=INPROJ_ROWS, masked_rows=0)

    ssd_params = [_lane_bcast(dt_bias), _lane_bcast(a_log),
                  _lane_bcast(jnp.repeat(d_skip, SSD_HEADDIM)), _lane_bcast(ssd_norm_g)]
    lam_init = 0.8 - 0.6 * math.exp(-0.3 * l)
    y_att, y_ssd = _mixers(q, k, vt, k_meta, vt_meta, lam_vecs.astype(F32),
                           _lane_bcast(subln_g, ATTN_TILE), xbct, dtt, zt, xbct_meta, dtt_meta,
                           ssd_params, lam_init=lam_init)

    consts = [gate_bias.astype(F32)[None, :], w_ssd_branch.astype(BF16),
              w_attn_branch.astype(BF16), w_out.astype(BF16), norm_ffn_g.astype(F32)[None, :],
              w_gate_ffn.astype(BF16), w_up_ffn.astype(BF16), w_down_ffn.astype(BF16),
              norm_final_g.astype(F32)[None, :]]
    out = _merge_ffn(x.reshape(nb * seq, d), y_ssd.reshape(nb * seq, SSD_INNER),
                     y_att.reshape(nb * seq, ATT_V), gates.reshape(nb * seq, 2 * d), consts)
    return out.reshape(nb, seq, d)


def kernel(x, meta_tokens, norm_mix_g, w_in, gate_bias, conv_w, conv_b, dt_bias, a_log, d_skip, ssd_norm_g, lambda_q1, lambda_k1, lambda_q2, lambda_k2, subln_g, w_ssd_branch, w_attn_branch, w_out, norm_ffn_g, w_gate_ffn, w_up_ffn, w_down_ffn, norm_final_g):
    depth = w_in.shape[0]
    assert depth == 1, "the fused final norm assumes a single layer"
    d = x.shape[-1]
    meta_chunk = jnp.concatenate(
        [jnp.zeros((PAD_FRONT, d), x.dtype), meta_tokens.astype(x.dtype)], axis=0)[None]
    l = 0
    lam_vecs = jnp.stack([lambda_q1[l], lambda_k1[l], lambda_q2[l], lambda_k2[l]])
    return _layer(x, meta_chunk, l, norm_mix_g[l], w_in[l], gate_bias[l], conv_w[l], conv_b[l],
                  dt_bias[l], a_log[l], d_skip[l], ssd_norm_g[l], lam_vecs, subln_g[l],
                  w_ssd_branch[l], w_attn_branch[l], w_out[l], norm_ffn_g[l], w_gate_ffn[l],
                  w_up_ffn[l], w_down_ffn[l], norm_final_g)
```

```python
import functools
import math

import jax
import jax.numpy as jnp
import numpy as np
from jax import lax
from jax.experimental import pallas as pl
from jax.experimental.pallas import tpu as pltpu

F32 = jnp.float32
BF16 = jnp.bfloat16

N_META = 16
BLOCK = 128
PAD_FRONT = BLOCK - N_META
NORM_EPS = 1e-6
NEG = -1e30

SSD_HEADDIM = 64
SSD_HEADS = 16
SSD_INNER = SSD_HEADS * SSD_HEADDIM
SSD_GROUPS = 2
SSD_STATE = 128
SSD_CONV = 4
SSD_BC = SSD_GROUPS * SSD_STATE
SSD_CONV_DIM = SSD_INNER + 2 * SSD_BC
HEADS_PER_GROUP = SSD_HEADS // SSD_GROUPS
GROUP_INNER = HEADS_PER_GROUP * SSD_HEADDIM

ATT_HEADS = 8
ATT_HEAD_DIM = 64
ATT_QK = ATT_HEADS * 2 * ATT_HEAD_DIM
ATT_V = ATT_HEADS * 2 * ATT_HEAD_DIM
ATT_HEAD_COLS = 2 * ATT_HEAD_DIM

LOG2E = math.log2(math.e)
Q_SCALE = ATT_HEAD_DIM ** -0.5 * LOG2E

LANES = 128
VMEM_LIMIT_BYTES = 56 * 1024 * 1024
INPROJ_ROWS = 512
CONV_ROWS = 64
ATTN_TILE = 512
ATTN_RUN = 8
ATTN_GROUP = 8
POS_RADIX = 16
DENOM_ROWS = 16
FFN_ROWS = 512


def _resident(shape):
    zeros = (0,) * len(shape)
    return pl.BlockSpec(shape, lambda *_: zeros, pipeline_mode=pl.Buffered(1))


def _sigmoid(x):
    return 1.0 / (1.0 + jnp.exp2(x * (-LOG2E)))


def _silu(x):
    return x * _sigmoid(x)


def _dot(a, b):
    return jnp.dot(a, b, preferred_element_type=F32)


def _dot_nt(a, b):
    return lax.dot_general(a, b, (((1,), (1,)), ((), ())), preferred_element_type=F32)


def _inproj_kernel(x_ref, g_ref, wq_ref, wk_ref, wg_ref, wxbct_ref, wzt_ref, wvt_ref, wdtt_ref,
                   convw_ref, convb_ref, hist_ref, q_ref, k_ref, gates_ref, xbct_ref, zt_ref,
                   vt_ref, dtt_ref, tail_ref, raw_ref, *, masked_rows):
    rows = x_ref.shape[1]
    x = x_ref[0]
    ms = jnp.mean(x * x, axis=-1, keepdims=True)
    u = x * lax.rsqrt(ms + NORM_EPS) * g_ref[...]
    if masked_rows:
        row = lax.broadcasted_iota(jnp.int32, u.shape, 0)
        u = jnp.where(row >= masked_rows, u, 0.0)
    ub = u.astype(BF16)

    @pl.when(pl.program_id(1) == 0)
    def _():
        raw_ref[:, 0:BLOCK] = hist_ref[0]

    raw_ref[:, BLOCK:] = _dot_nt(wxbct_ref[...], ub)
    q_ref[0] = (_dot(ub, wq_ref[...]) * Q_SCALE).astype(q_ref.dtype)
    k_ref[0] = _dot(ub, wk_ref[...]).astype(k_ref.dtype)
    gates_ref[0] = _dot(ub, wg_ref[...]).astype(gates_ref.dtype)
    zt_ref[0] = _dot_nt(wzt_ref[...], ub).astype(zt_ref.dtype)
    vt_ref[0] = _dot_nt(wvt_ref[...], ub).astype(vt_ref.dtype)
    dtt_ref[0] = _dot_nt(wdtt_ref[...], ub)
    lane = lax.broadcasted_iota(jnp.int32, (CONV_ROWS, BLOCK), 1)
    for blk in range(SSD_CONV_DIM // CONV_ROWS):
        rs = slice(blk * CONV_ROWS, (blk + 1) * CONV_ROWS)
        taps = [convw_ref[t, rs, :] for t in range(SSD_CONV)]
        bias = convb_ref[rs, :]
        for cb in range(rows // BLOCK):
            x_prev = raw_ref[rs, cb * BLOCK:(cb + 1) * BLOCK]
            xc = raw_ref[rs, (cb + 1) * BLOCK:(cb + 2) * BLOCK]
            acc = bias + taps[SSD_CONV - 1] * xc
            for j in range(1, SSD_CONV):
                joined = jnp.where(lane >= BLOCK - j, x_prev, xc)
                acc = acc + taps[SSD_CONV - 1 - j] * pltpu.roll(joined, j, 1)
            xbct_ref[0, rs, cb * BLOCK:(cb + 1) * BLOCK] = _silu(acc)
        tail = raw_ref[rs, rows:rows + BLOCK]
        raw_ref[rs, 0:BLOCK] = tail
        tail_ref[0, rs, :] = tail


def _in_projection(h, g, w, conv_w, conv_b, hist, *, rows, masked_rows):
    nb, seq, d = h.shape
    grid = (nb, seq // rows)
    row_major = lambda cols, dt: (jax.ShapeDtypeStruct((nb, seq, cols), dt),
                                  pl.BlockSpec((1, rows, cols), lambda b, i: (b, i, 0)))
    chan_major = lambda cols, dt: (jax.ShapeDtypeStruct((nb, cols, seq), dt),
                                   pl.BlockSpec((1, cols, rows), lambda b, i: (b, 0, i)))
    tail = (jax.ShapeDtypeStruct((nb, SSD_CONV_DIM, BLOCK), F32),
            pl.BlockSpec((1, SSD_CONV_DIM, BLOCK), lambda b, i: (b, 0, 0)))
    outs = [row_major(ATT_QK, BF16), row_major(ATT_QK, BF16), row_major(2 * d, BF16),
            chan_major(SSD_CONV_DIM, F32), chan_major(SSD_INNER, BF16), chan_major(ATT_V, BF16),
            chan_major(SSD_HEADS, F32), tail]
    consts = [g, w["q"], w["k"], w["gates"], w["xbc_t"], w["z_t"], w["v_t"], w["dt_t"],
              conv_w, conv_b, hist]
    return pl.pallas_call(
        functools.partial(_inproj_kernel, masked_rows=masked_rows),
        grid=grid,
        in_specs=[pl.BlockSpec((1, rows, d), lambda b, i: (b, i, 0))]
                 + [_resident(a.shape) for a in consts],
        out_specs=[o[1] for o in outs],
        out_shape=[o[0] for o in outs],
        scratch_shapes=[pltpu.VMEM((SSD_CONV_DIM, BLOCK + rows), F32)],
        compiler_params=pltpu.CompilerParams(
            dimension_semantics=("parallel", "arbitrary"), vmem_limit_bytes=VMEM_LIMIT_BYTES),
        name="in_projection",
    )(h, *consts)


def _ssd_scan(xbc_ref, dtt_ref, ts, zt_ref, y_ref, dtb_ref, alog_ref, dskip_ref, ng_ref,
              state_ref, ygate_ref, *, first):
    v = dtt_ref[0, :, ts] + dtb_ref[...]
    dt = jnp.maximum(v, 0.0) + jnp.log1p(jnp.exp(-jnp.abs(v)))
    if first:
        step = lax.broadcasted_iota(jnp.int32, dt.shape, 1)
        dt = jnp.where(step >= PAD_FRONT, dt, 0.0)
    da = dt * (-jnp.exp(alog_ref[...]))
    r = lax.broadcasted_iota(jnp.int32, (BLOCK, BLOCK), 0)
    c = lax.broadcasted_iota(jnp.int32, (BLOCK, BLOCK), 1)
    upper = r <= c
    acs = jnp.dot(da, upper.astype(F32), preferred_element_type=F32,
                  precision=lax.Precision.HIGHEST)
    last = acs[:, BLOCK - 1:BLOCK]
    to_end = jnp.exp(last - acs) * dt
    chunk_decay = jnp.broadcast_to(jnp.exp(last), acs.shape)
    if not first:
        acs_col = jnp.concatenate([acs, jnp.zeros((BLOCK - SSD_HEADS, BLOCK), F32)], axis=0).T
        grow = jnp.exp(acs)
        ssq = jnp.zeros((1, BLOCK), F32)

    for g in range(SSD_GROUPS):
        b_rows = slice(SSD_INNER + g * SSD_STATE, SSD_INNER + (g + 1) * SSD_STATE)
        c_rows = slice(SSD_INNER + SSD_BC + g * SSD_STATE, SSD_INNER + SSD_BC + (g + 1) * SSD_STATE)
        bg_t = xbc_ref[0, b_rows, ts].T.astype(BF16)
        heads = range(g * HEADS_PER_GROUP, (g + 1) * HEADS_PER_GROUP)
        grp = slice(g * GROUP_INNER, (g + 1) * GROUP_INNER)
        per_row = lambda a: jnp.concatenate(
            [jnp.broadcast_to(a[h:h + 1, :], (SSD_HEADDIM, BLOCK)) for h in heads], axis=0)
        inject = _dot((xbc_ref[0, grp, ts] * per_row(to_end)).astype(BF16), bg_t)
        if first:
            state_ref[grp, :] = inject
            continue
        cg = xbc_ref[0, c_rows, ts].astype(BF16)
        cb_t = _dot(bg_t, cg)
        st = state_ref[grp, :]
        state_ref[grp, :] = st * per_row(chunk_decay) + inject
        y_off = _dot(st.astype(BF16), cg)
        for e, h in enumerate(heads):
            rows = slice(h * SSD_HEADDIM, (h + 1) * SSD_HEADDIM)
            xs = xbc_ref[0, rows, ts]
            seg = acs[h:h + 1, :] - acs_col[:, h:h + 1]
            g_t = (cb_t * jnp.where(upper, jnp.exp(seg), 0.0)).astype(BF16)
            y = _dot((xs * dt[h:h + 1, :]).astype(BF16), g_t)
            y = (y + y_off[e * SSD_HEADDIM:(e + 1) * SSD_HEADDIM] * grow[h:h + 1, :]
                 + xs * dskip_ref[rows, :])
            y = y * _silu(zt_ref[0, rows, ts].astype(F32))
            ssq = ssq + jnp.sum(y * y, axis=0, keepdims=True)
            ygate_ref[rows, :] = y

    if not first:
        scale = lax.rsqrt(ssq * (1.0 / SSD_INNER) + NORM_EPS)
        for blk in range(SSD_INNER // BLOCK):
            rs = slice(blk * BLOCK, (blk + 1) * BLOCK)
            y = ygate_ref[rs, :] * scale * ng_ref[rs, :]
            y_ref[0, ts, rs] = y.T.astype(y_ref.dtype)


def _mixer_kernel(q_ref, k_ref, vt_ref, k_meta_ref, vt_meta_ref, slope_ref, coef_ref, pos_ref,
                  lam_ref, g_ref, xbct_ref, dtt_ref, zt_ref, xbct_meta_ref, dtt_meta_ref,
                  dtb_ref, alog_ref, dskip_ref, ng_ref, o_ref, y_ref, qa_ref, m_ref, acc_ref,
                  sa_ref, ca_ref, sb_ref, cb_ref, state_ref, ygate_ref, *, lam_init, n_tiles):
    j = pl.program_id(2)
    scan = functools.partial(_ssd_scan, zt_ref=zt_ref, y_ref=y_ref, dtb_ref=dtb_ref,
                             alog_ref=alog_ref, dskip_ref=dskip_ref, ng_ref=ng_ref,
                             state_ref=state_ref, ygate_ref=ygate_ref)

    @pl.when((pl.program_id(1) == 0) & (j == 0))
    def _():
        scan(xbct_meta_ref, dtt_meta_ref, slice(0, BLOCK), first=True)

    t = ATTN_TILE
    hc = ATT_HEAD_COLS
    group = ATTN_GROUP
    steps = n_tiles // group
    q_tiles = []
    for pair in range(group // 2):
        q_tiles += [j + pair * steps, n_tiles - 1 - j - pair * steps]
    slope = slope_ref[0][:, 0:1]
    key_row = lax.broadcasted_iota(jnp.int32, (t, t), 0)
    qry_col = lax.broadcasted_iota(jnp.int32, (t, t), 1)
    meta_row = lax.broadcasted_iota(jnp.int32, (BLOCK, t), 0)

    def rows(tile):
        return pl.ds(pl.multiple_of(tile * t, t), t)

    coef = coef_ref[0]
    chan = lax.broadcasted_iota(jnp.int32, (hc, t), 0)
    for w, tile in enumerate(q_tiles):
        q_t = q_ref[0, rows(tile), :].astype(F32).T
        for comp, keep in enumerate((chan < ATT_HEAD_DIM, chan >= ATT_HEAD_DIM)):
            qa_ref[w, comp] = jnp.concatenate(
                [jnp.where(keep, q_t, 0.0).astype(BF16), coef], axis=0)
    m_ref[...] = jnp.full(m_ref.shape, NEG, F32)
    acc_ref[...] = jnp.zeros(acc_ref.shape, F32)

    half = t // 2

    def scores(kt, w, mask, diagonal=False, comps=(0, 1)):
        k_aug = jnp.concatenate([kt, pos_ref[0:kt.shape[0], :]], axis=1)
        out = []
        for comp in comps:
            if diagonal:
                late = jnp.concatenate([jnp.zeros((half, half), F32),
                                        _dot(k_aug[half:], qa_ref[w, comp, :, half:])], axis=1)
                s = jnp.concatenate([_dot(k_aug[:half], qa_ref[w, comp]), late], axis=0)
            else:
                s = _dot(k_aug, qa_ref[w, comp])
            out.append(s if mask is None else jnp.where(mask, s, NEG))
        return out

    def fold(w, s_of, cmax_of, vtt, key_start, diagonal=False, comps=(0, 1)):
        sigma = slope * lax.convert_element_type(key_start, F32)
        v_aug = jnp.concatenate([vtt, jnp.ones((DENOM_ROWS, vtt.shape[1]), BF16)], axis=0)
        for comp in comps:
            m_old = m_ref[w, comp]
            m_new = jnp.maximum(m_old, cmax_of(comp) + sigma)
            alpha = jnp.exp2(m_old - m_new)
            pb = jnp.exp2(s_of(comp) - (m_new - sigma)).astype(BF16)
            if diagonal:
                update = jnp.concatenate([_dot(v_aug[:, :half], pb[:half, :half]),
                                          _dot(v_aug, pb[:, half:])], axis=1)
            else:
                update = _dot(v_aug, pb)
            acc_ref[w, comp] = alpha * acc_ref[w, comp] + update
            m_ref[w, comp] = m_new

    n_full = (group // 2) * (n_tiles - 1)
    n_items = group + n_full
    lead = (n_items - 1) % ATTN_RUN
    assert ATTN_RUN % 2 == 0 and group - 1 <= ATTN_RUN and n_items - 1 >= ATTN_RUN + lead
    starts = [sum(q_tiles[:w]) for w in range(group)]

    def pick(w, values):
        out = values[0]
        for v in range(1, group):
            out = jnp.where(w == v, values[v], out)
        return out

    def item(r):
        r = jnp.asarray(r, jnp.int32)
        w_full = sum((r - 1 >= starts[v]).astype(jnp.int32) for v in range(1, group))
        w = jnp.where(r == 0, 0, jnp.where(r > n_full, r - n_full, w_full))
        on_diagonal = (r == 0) | (r > n_full)
        key_tile = jnp.where(on_diagonal, pick(w, q_tiles), r - 1 - pick(w_full, starts))
        return w, key_tile, pick(w, q_tiles)

    slots = ((sa_ref, ca_ref), (sb_ref, cb_ref))

    def stage(r, parity, masked, comps=(0, 1)):
        w, key_tile, _ = item(r)
        stage_ref, cmax_ref = slots[parity]
        mask = (key_row <= qry_col) if masked else None
        for comp, s in zip(comps, scores(k_ref[0, rows(key_tile), :], w, mask, masked, comps)):
            stage_ref[comp] = s
            cmax_ref[comp] = jnp.max(s, axis=0, keepdims=True)

    def fold_staged(r, parity, comps=(0, 1)):
        w, key_tile, q_tile = item(r)
        stage_ref, cmax_ref = slots[parity]
        fold(w, lambda comp: stage_ref[comp], lambda comp: cmax_ref[comp],
             vt_ref[0, :, rows(key_tile)], (key_tile - q_tile) * t,
             diagonal=isinstance(r, int) and (r == 0 or r > n_full), comps=comps)

    s_meta = [scores(k_meta_ref[0], w, meta_row >= PAD_FRONT) for w in range(group)]
    stage(0, 0, True)
    for w, tile in enumerate(q_tiles):
        fold(w, lambda comp: s_meta[w][comp],
             lambda comp: jnp.max(s_meta[w][comp], axis=0, keepdims=True),
             vt_meta_ref[0], -(BLOCK + tile * t))

    def steps_from(base, base_parity, count):
        for c in range(1, count + 1):
            for comps in ((0,), (1,)):
                stage(base + c, (base_parity + c) % 2,
                      isinstance(base, int) and base + c > n_full, comps)
                fold_staged(base + c - 1, (base_parity + c - 1) % 2, comps)

    def run_body(u, carry):
        steps_from(lead + u * ATTN_RUN, lead % 2, ATTN_RUN)
        return carry

    if lead:
        steps_from(0, 0, lead)
    n_runs = (n_items - 1 - lead) // ATTN_RUN
    lax.fori_loop(0, n_runs - 1, run_body, 0)
    for sub in range(group // 2):
        scan(xbct_ref, dtt_ref, slice(sub * BLOCK, (sub + 1) * BLOCK), first=False)
    steps_from(lead + (n_runs - 1) * ATTN_RUN, lead % 2, ATTN_RUN)
    fold_staged(n_items - 1, (n_items - 1) % 2)

    lam = (jnp.exp(jnp.sum(lam_ref[0:1] * lam_ref[1:2], axis=-1, keepdims=True))
           - jnp.exp(jnp.sum(lam_ref[2:3] * lam_ref[3:4], axis=-1, keepdims=True)) + lam_init)
    for w, tile in enumerate(q_tiles):
        out = [acc_ref[w, comp, 0:hc, :] * (1.0 / acc_ref[w, comp, hc:hc + 1, :])
               for comp in range(2)]
        o = out[0] - lam * out[1]
        ms = jnp.mean(o * o, axis=0, keepdims=True)
        y = o * lax.rsqrt(ms + NORM_EPS) * g_ref[...] * (1.0 - lam_init)
        o_ref[0, rows(tile), :] = y.T.astype(o_ref.dtype)


def _alibi_operands():
    heads = np.arange(ATT_HEADS, dtype=np.float32)
    slope = (2.0 ** (-8.0 * (heads + 1.0) / ATT_HEADS)).astype(np.float32) * np.float32(LOG2E)
    pieces, rest = [], slope
    for _ in range(3):
        piece = rest.astype(BF16).astype(np.float32)
        pieces.append(piece)
        rest = rest - piece
    cols = [np.float32(POS_RADIX) * p for p in pieces] + pieces
    coef = np.zeros((ATT_HEADS, LANES, ATTN_TILE), np.float32)
    coef[:, :len(cols), :] = np.stack(cols, axis=1)[:, :, None]
    row = np.arange(ATTN_TILE)
    pos = np.zeros((ATTN_TILE, LANES), np.float32)
    pos[:, 0:3] = (row // POS_RADIX)[:, None]
    pos[:, 3:6] = (row % POS_RADIX)[:, None]
    slope = np.broadcast_to(slope[:, None, None], (ATT_HEADS, 1, LANES))
    return jnp.asarray(slope), jnp.asarray(coef, dtype=BF16), jnp.asarray(pos, dtype=BF16)


def _mixers(q, k, vt, k_meta, vt_meta, lam_vecs, subln_g, xbct, dtt, zt, xbct_meta, dtt_meta,
            ssd_params, *, lam_init):
    nb, seq, _ = q.shape
    t = ATTN_TILE
    hc = ATT_HEAD_COLS
    n_tiles = seq // t
    group = ATTN_GROUP
    steps = n_tiles // group
    span = BLOCK * group // 2
    assert n_tiles % group == 0 and ATT_HEADS * steps * span == seq
    slopes, coef, pos = _alibi_operands()
    per_head_rows = pl.BlockSpec((1, seq, hc), lambda b, h, j: (b, 0, h))
    chunk = lambda rows: pl.BlockSpec((1, rows, span), lambda b, h, j: (b, 0, h * steps + j))
    return pl.pallas_call(
        functools.partial(_mixer_kernel, lam_init=lam_init, n_tiles=n_tiles),
        grid=(nb, ATT_HEADS, steps),
        in_specs=[per_head_rows, per_head_rows,
                  pl.BlockSpec((1, hc, seq), lambda b, h, j: (b, h, 0)),
                  pl.BlockSpec((1, BLOCK, hc), lambda b, h, j: (0, 0, h)),
                  pl.BlockSpec((1, hc, BLOCK), lambda b, h, j: (0, h, 0)),
                  pl.BlockSpec((1, 1, LANES), lambda b, h, j: (h, 0, 0)),
                  pl.BlockSpec((1, LANES, t), lambda b, h, j: (h, 0, 0)),
                  _resident(pos.shape), _resident(lam_vecs.shape), _resident(subln_g.shape),
                  chunk(SSD_CONV_DIM), chunk(SSD_HEADS), chunk(SSD_INNER),
                  _resident(xbct_meta.shape), _resident(dtt_meta.shape)]
                 + [_resident(a.shape) for a in ssd_params],
        out_specs=[per_head_rows,
                   pl.BlockSpec((1, span, SSD_INNER), lambda b, h, j: (b, h * steps + j, 0))],
        out_shape=[jax.ShapeDtypeStruct((nb, seq, ATT_V), BF16),
                   jax.ShapeDtypeStruct((nb, seq, SSD_INNER), BF16)],
        scratch_shapes=[pltpu.VMEM((group, 2, 2 * LANES, t), BF16),
                        pltpu.VMEM((group, 2, 1, t), F32),
                        pltpu.VMEM((group, 2, hc + DENOM_ROWS, t), F32)]
                       + [pltpu.VMEM((2, t, t), F32), pltpu.VMEM((2, 1, t), F32)] * 2
                       + [pltpu.VMEM((SSD_INNER, SSD_STATE), F32),
                          pltpu.VMEM((SSD_INNER, BLOCK), F32)],
        compiler_params=pltpu.CompilerParams(
            dimension_semantics=("parallel", "arbitrary", "arbitrary"),
            vmem_limit_bytes=VMEM_LIMIT_BYTES),
        name="mixers",
    )(q, k, vt, k_meta, vt_meta, slopes, coef, pos, lam_vecs, subln_g, xbct, dtt, zt,
      xbct_meta, dtt_meta, *ssd_params)


def _merge_ffn_kernel(x_ref, ys_ref, ya_ref, gates_ref, gbias_ref, ws_ref, wa_ref, wo_ref,
                      nffn_ref, wg_ref, wu_ref, wd_ref, nfin_ref, o_ref):
    d = x_ref.shape[-1]
    gates = _sigmoid(gates_ref[...].astype(F32) + gbias_ref[...])
    merged = (gates[:, :d] * _dot(ys_ref[...], ws_ref[...])
              + gates[:, d:] * _dot(ya_ref[...], wa_ref[...]))
    h = x_ref[...] + _dot(merged.astype(BF16), wo_ref[...])
    u = h * lax.rsqrt(jnp.mean(h * h, axis=-1, keepdims=True) + NORM_EPS) * nffn_ref[...]
    ub = u.astype(BF16)
    hidden = _silu(_dot(ub, wg_ref[...])) * _dot(ub, wu_ref[...])
    h = h + _dot(hidden.astype(BF16), wd_ref[...])
    o_ref[...] = h * lax.rsqrt(jnp.mean(h * h, axis=-1, keepdims=True) + NORM_EPS) * nfin_ref[...]


def _merge_ffn(x2, ys, ya, gates, consts):
    n, d = x2.shape
    rows = FFN_ROWS
    tile = lambda cols: pl.BlockSpec((rows, cols), lambda i: (i, 0))
    gbias, ws, wa, wo, nffn, wg, wu, wd, nfin = consts
    return pl.pallas_call(
        _merge_ffn_kernel,
        grid=(n // rows,),
        in_specs=[tile(d), tile(ys.shape[1]), tile(ya.shape[1]), tile(gates.shape[1])]
                 + [_resident(a.shape) for a in consts],
        out_specs=tile(d),
        out_shape=jax.ShapeDtypeStruct((n, d), F32),
        compiler_params=pltpu.CompilerParams(
            dimension_semantics=("parallel",), vmem_limit_bytes=VMEM_LIMIT_BYTES),
        name="merge_ffn",
    )(x2, ys, ya, gates, *consts)


def _lane_bcast(v, n=LANES):
    return jnp.broadcast_to(v.astype(F32)[:, None], (v.shape[0], n))


def _layer(x, meta_chunk, l, norm_mix_g, w_in, gate_bias, conv_w, conv_b, dt_bias, a_log, d_skip,
           ssd_norm_g, lam_vecs, subln_g, w_ssd_branch, w_attn_branch, w_out, norm_ffn_g,
           w_gate_ffn, w_up_ffn, w_down_ffn, norm_final_g):
    nb, seq, d = x.shape
    sizes = [SSD_INNER, SSD_CONV_DIM, SSD_HEADS, ATT_QK, ATT_QK, ATT_V, 2 * d]
    offs = [0]
    for s in sizes:
        offs.append(offs[-1] + s)
    col = lambda idx: w_in[:, offs[idx]:offs[idx + 1]].astype(BF16)
    w = {"z_t": col(0).T, "xbc_t": col(1).T, "dt_t": col(2).T, "q": col(3), "k": col(4),
         "v_t": col(5).T, "gates": col(6)}
    g_mix = norm_mix_g.astype(F32)[None, :]

    conv_taps = jnp.stack([_lane_bcast(conv_w[:, t]) for t in range(SSD_CONV)])
    conv_bias = _lane_bcast(conv_b)
    _, k_meta, _, xbct_meta, _, vt_meta, dtt_meta, meta_tail = _in_projection(
        meta_chunk, g_mix, w, conv_taps, conv_bias, jnp.zeros((1, SSD_CONV_DIM, BLOCK), F32),
        rows=BLOCK, masked_rows=PAD_FRONT)
    q, k, gates, xbct, zt, vt, dtt, _ = _in_projection(
        x, g_mix, w, conv_taps, conv_bias, meta_tail, rows=INPROJ_ROWS, masked_rows=0)

    ssd_params = [_lane_bcast(dt_bias), _lane_bcast(a_log),
                  _lane_bcast(jnp.repeat(d_skip, SSD_HEADDIM)), _lane_bcast(ssd_norm_g)]
    lam_init = 0.8 - 0.6 * math.exp(-0.3 * l)
    y_att, y_ssd = _mixers(q, k, vt, k_meta, vt_meta, lam_vecs.astype(F32),
                           _lane_bcast(subln_g, ATTN_TILE), xbct, dtt, zt, xbct_meta, dtt_meta,
                           ssd_params, lam_init=lam_init)

    consts = [gate_bias.astype(F32)[None, :], w_ssd_branch.astype(BF16),
              w_attn_branch.astype(BF16), w_out.astype(BF16), norm_ffn_g.astype(F32)[None, :],
              w_gate_ffn.astype(BF16), w_up_ffn.astype(BF16), w_down_ffn.astype(BF16),
              norm_final_g.astype(F32)[None, :]]
    out = _merge_ffn(x.reshape(nb * seq, d), y_ssd.reshape(nb * seq, SSD_INNER),
                     y_att.reshape(nb * seq, ATT_V), gates.reshape(nb * seq, 2 * d), consts)
    return out.reshape(nb, seq, d)


def kernel(x, meta_tokens, norm_mix_g, w_in, gate_bias, conv_w, conv_b, dt_bias, a_log, d_skip, ssd_norm_g, lambda_q1, lambda_k1, lambda_q2, lambda_k2, subln_g, w_ssd_branch, w_attn_branch, w_out, norm_ffn_g, w_gate_ffn, w_up_ffn, w_down_ffn, norm_final_g):
    depth = w_in.shape[0]
    assert depth == 1, "the fused final norm assumes a single layer"
    d = x.shape[-1]
    meta_chunk = jnp.concatenate(
        [jnp.zeros((PAD_FRONT, d), x.dtype), meta_tokens.astype(x.dtype)], axis=0)[None]
    l = 0
    lam_vecs = jnp.stack([lambda_q1[l], lambda_k1[l], lambda_q2[l], lambda_k2[l]])
    return _layer(x, meta_chunk, l, norm_mix_g[l], w_in[l], gate_bias[l], conv_w[l], conv_b[l],
                  dt_bias[l], a_log[l], d_skip[l], ssd_norm_g[l], lam_vecs, subln_g[l],
                  w_ssd_branch[l], w_attn_branch[l], w_out[l], norm_ffn_g[l], w_gate_ffn[l],
                  w_up_ffn[l], w_down_ffn[l], norm_final_g)
```
